```python
import jax, jax.numpy as jnp
from jax import lax
import numpy as np

D_MODEL = 1024
BATCH = 4
SEQ = 8192
DEPTH = 1

HEAD_DIM = 64
N_SWA_HEADS = 8
N_SWA_KV = 2
WINDOW = 128
BLOCK = 128
N_DSA_HEADS = 4
N_IDX_HEADS = 4
IDX_DIM = 64
TOPK_MAX = 256
N_MEM_HEADS = 4
MEM_LEN = 256
N_BRANCH = 3
D_FF = 4 * D_MODEL
ROPE_THETA = 10000.0
EPS = 1e-6

SWA_WIDTH = N_SWA_HEADS * HEAD_DIM
SWA_KV_WIDTH = N_SWA_KV * HEAD_DIM
DSA_WIDTH = N_DSA_HEADS * HEAD_DIM
MEM_WIDTH = N_MEM_HEADS * HEAD_DIM
IN_SPLITS = (SWA_WIDTH, SWA_KV_WIDTH, SWA_KV_WIDTH,
             DSA_WIDTH, HEAD_DIM, HEAD_DIM,
             N_IDX_HEADS * IDX_DIM, IDX_DIM, N_IDX_HEADS,
             MEM_WIDTH, N_BRANCH * D_MODEL)
D_IN = sum(IN_SPLITS)

kernel_name = "hybrid_swa_dsa_memory_gated_block"


def rmsnorm(x, g):
    xf = x.astype(jnp.float32)
    y = xf * lax.rsqrt(jnp.mean(xf * xf, axis=-1, keepdims=True) + EPS)
    return (y * g.astype(jnp.float32)).astype(x.dtype)


def rope(x, pos):
    half = x.shape[-1] // 2
    inv_freq = jnp.power(ROPE_THETA, -jnp.arange(half, dtype=jnp.float32) / half)
    ang = pos.astype(jnp.float32)[..., None] * inv_freq
    cos = jnp.cos(ang)[:, :, None, :].astype(x.dtype)
    sin = jnp.sin(ang)[:, :, None, :].astype(x.dtype)
    x1, x2 = x[..., :half], x[..., half:]
    return jnp.concatenate([x1 * cos - x2 * sin, x2 * cos + x1 * sin], axis=-1)


def sliding_window_attention(q, k, v, sinks):
    B, S, Hq, D = q.shape
    Hkv = k.shape[2]
    G = Hq // Hkv
    nb = S // BLOCK
    qb = q.reshape(B, nb, BLOCK, Hkv, G, D)

    def with_prev(t):
        tb = t.reshape(B, nb, BLOCK, Hkv, D)
        prev = jnp.concatenate([jnp.zeros_like(tb[:, :1]), tb[:, :-1]], axis=1)
        return jnp.concatenate([prev, tb], axis=2)

    kw, vw = with_prev(k), with_prev(v)
    scores = jnp.einsum("bnqkgd,bnskd->bnkgqs", qb, kw).astype(jnp.float32) * (D ** -0.5)
    qi = jnp.arange(BLOCK)[:, None]
    sj = jnp.arange(2 * BLOCK)[None, :]
    dist = qi + BLOCK - sj
    blk = jnp.arange(nb)[:, None, None]
    valid = (dist >= 0) & (dist < WINDOW) & ((blk > 0) | (sj >= BLOCK))
    scores = jnp.where(valid[None, :, None, None], scores, -jnp.inf)
    sink = jnp.broadcast_to(
        sinks.reshape(Hkv, G)[None, None, :, :, None, None].astype(jnp.float32),
        scores.shape[:-1] + (1,))
    probs = jax.nn.softmax(jnp.concatenate([scores, sink], axis=-1), axis=-1)[..., :-1]
    out = jnp.einsum("bnkgqs,bnskd->bnqkgd", probs.astype(vw.dtype), vw)
    return out.reshape(B, S, Hq * D)


def dsa_attention(q, k, v, q_idx, k_idx, w_idx):
    B, S = q.shape[:2]
    nb = S // BLOCK
    top_k = min(TOPK_MAX, S // 4)
    key_pos = jnp.arange(S)
    gather = jax.vmap(lambda t, i: t[i])

    def to_blocks(t):
        return jnp.moveaxis(t.reshape((B, nb, BLOCK) + t.shape[2:]), 1, 0)

    def block_fn(args):
        qb, qib, wb, blk = args
        qpos = blk * BLOCK + jnp.arange(BLOCK)
        rel = jax.nn.relu(jnp.einsum("bqhd,bsd->bqhs", qib, k_idx).astype(jnp.float32))
        iscore = jnp.einsum("bqhs,bqh->bqs", rel, wb.astype(jnp.float32))
        iscore = jnp.where(key_pos[None, None, :] <= qpos[None, :, None], iscore, -jnp.inf)
        _, sel = lax.top_k(iscore, top_k)
        kg = gather(k, sel)
        vg = gather(v, sel)
        s = jnp.einsum("bqhd,bqkd->bhqk", qb, kg).astype(jnp.float32) * (HEAD_DIM ** -0.5)
        s = jnp.where((sel <= qpos[None, :, None])[:, None], s, -jnp.inf)
        p = jax.nn.softmax(s, axis=-1)
        return jnp.einsum("bhqk,bqkd->bqhd", p.astype(vg.dtype), vg)

    out = lax.map(block_fn, (to_blocks(q), to_blocks(q_idx), to_blocks(w_idx), jnp.arange(nb)))
    return jnp.moveaxis(out, 0, 1).reshape(B, S, -1)


def memory_cross_attention(q, mem_k, mem_v):
    B, S = q.shape[:2]
    s = jnp.einsum("bqhd,bmhd->bhqm", q, mem_k).astype(jnp.float32) * (HEAD_DIM ** -0.5)
    p = jax.nn.softmax(s, axis=-1)
    return jnp.einsum("bhqm,bmhd->bqhd", p.astype(mem_v.dtype), mem_v).reshape(B, S, -1)


def setup_inputs(seed: int = 0) -> dict:
    key = jax.random.key(seed)
    ks = jax.random.split(key, 20)
    f32 = jnp.float32

    def nrm(k, shape, scale):
        return jax.random.normal(k, shape, f32) * scale

    def gain(k, shape):
        return 1.0 + 0.02 * jax.random.normal(k, shape, f32)

    return {
        "x": nrm(ks[0], (BATCH, SEQ, D_MODEL), 1.0),
        "mem": nrm(ks[1], (BATCH, MEM_LEN, D_MODEL), 1.0),
        "positions": jnp.broadcast_to(jnp.arange(SEQ, dtype=jnp.int32)[None], (BATCH, SEQ)),
        "g_mix": gain(ks[2], (DEPTH, D_MODEL)),
        "w_in": nrm(ks[3], (DEPTH, D_MODEL, D_IN), D_MODEL ** -0.5),
        "b_gate": nrm(ks[4], (DEPTH, N_BRANCH * D_MODEL), 0.01),
        "sinks": nrm(ks[5], (DEPTH, N_SWA_HEADS), 0.5),
        "g_mem": gain(ks[6], (DEPTH, D_MODEL)),
        "w_mem_kv": nrm(ks[7], (DEPTH, D_MODEL, 2 * MEM_WIDTH), D_MODEL ** -0.5),
        "w_proj_swa": nrm(ks[8], (DEPTH, SWA_WIDTH, D_MODEL), SWA_WIDTH ** -0.5),
        "w_proj_dsa": nrm(ks[9], (DEPTH, DSA_WIDTH, D_MODEL), DSA_WIDTH ** -0.5),
        "w_proj_mem": nrm(ks[10], (DEPTH, MEM_WIDTH, D_MODEL), MEM_WIDTH ** -0.5),
        "w_out": nrm(ks[11], (DEPTH, D_MODEL, D_MODEL), D_MODEL ** -0.5),
        "g_mlp": gain(ks[12], (DEPTH, D_MODEL)),
        "w_mlp_in": nrm(ks[13], (DEPTH, D_MODEL, D_FF), D_MODEL ** -0.5),
        "w_mlp_out": nrm(ks[14], (DEPTH, D_FF, D_MODEL), D_FF ** -0.5),
        "g_final": gain(ks[15], (D_MODEL,)),
    }


def reference(x, mem, positions, g_mix, w_in, b_gate, sinks, g_mem, w_mem_kv,
              w_proj_swa, w_proj_dsa, w_proj_mem, w_out, g_mlp, w_mlp_in, w_mlp_out,
              g_final):
    B, S, _ = x.shape
    M = mem.shape[1]
    split_points = np.cumsum(IN_SPLITS)[:-1].tolist()
    idx_scale = (N_IDX_HEADS ** -0.5) * (IDX_DIM ** -0.5)
    for l in range(DEPTH):
        h = rmsnorm(x, g_mix[l])
        proj = jnp.einsum("bsd,de->bse", h, w_in[l])
        (q_s, k_s, v_s, q_d, k_d, v_d, q_i, k_i, w_i, q_m, gate_logits) = jnp.split(
            proj, split_points, axis=-1)

        q_s = rope(q_s.reshape(B, S, N_SWA_HEADS, HEAD_DIM), positions)
        k_s = rope(k_s.reshape(B, S, N_SWA_KV, HEAD_DIM), positions)
        v_s = v_s.reshape(B, S, N_SWA_KV, HEAD_DIM)
        o_swa = sliding_window_attention(q_s, k_s, v_s, sinks[l])

        q_d = rope(q_d.reshape(B, S, N_DSA_HEADS, HEAD_DIM), positions)
        k_d = rope(k_d[:, :, None, :], positions)[:, :, 0]
        q_i = rope(q_i.reshape(B, S, N_IDX_HEADS, IDX_DIM), positions)
        k_i = rope(k_i[:, :, None, :], positions)[:, :, 0]
        o_dsa = dsa_attention(q_d, k_d, v_d, q_i, k_i, w_i * idx_scale)

        m = rmsnorm(mem, g_mem[l])
        mkv = jnp.einsum("bmd,de->bme", m, w_mem_kv[l]).reshape(B, M, 2, N_MEM_HEADS, HEAD_DIM)
        o_mem = memory_cross_attention(q_m.reshape(B, S, N_MEM_HEADS, HEAD_DIM),
                                       mkv[:, :, 0], mkv[:, :, 1])

        gates = jax.nn.sigmoid(gate_logits + b_gate[l]).reshape(B, S, N_BRANCH, D_MODEL)
        merged = (gates[:, :, 0] * jnp.einsum("bse,ed->bsd", o_swa, w_proj_swa[l])
                  + gates[:, :, 1] * jnp.einsum("bse,ed->bsd", o_dsa, w_proj_dsa[l])
                  + gates[:, :, 2] * jnp.einsum("bse,ed->bsd", o_mem, w_proj_mem[l]))
        x = x + jnp.einsum("bsd,de->bse", merged, w_out[l])

        h = rmsnorm(x, g_mlp[l])
        hid = jnp.square(jax.nn.relu(jnp.einsum("bsd,df->bsf", h, w_mlp_in[l])))
        x = x + jnp.einsum("bsf,fd->bsd", hid, w_mlp_out[l])
    return rmsnorm(x, g_final)
```

```python
import functools

import jax
import jax.numpy as jnp
import numpy as np
from jax import lax
from jax.experimental import pallas as pl
from jax.experimental.pallas import tpu as pltpu

D_MODEL = 1024
HEAD_DIM = 64
N_SWA_HEADS = 8
N_SWA_KV = 2
WINDOW = 128
BLOCK = 128
N_DSA_HEADS = 4
N_IDX_HEADS = 4
IDX_DIM = 64
TOPK_MAX = 256
N_MEM_HEADS = 4
N_BRANCH = 3
D_FF = 4 * D_MODEL
ROPE_THETA = 10000.0
EPS = 1e-6

SWA_WIDTH = N_SWA_HEADS * HEAD_DIM
SWA_KV_WIDTH = N_SWA_KV * HEAD_DIM
DSA_WIDTH = N_DSA_HEADS * HEAD_DIM
MEM_WIDTH = N_MEM_HEADS * HEAD_DIM
IN_SPLITS = (SWA_WIDTH, SWA_KV_WIDTH, SWA_KV_WIDTH,
             DSA_WIDTH, HEAD_DIM, HEAD_DIM,
             N_IDX_HEADS * IDX_DIM, IDX_DIM, N_IDX_HEADS,
             MEM_WIDTH, N_BRANCH * D_MODEL)

LANES = 128
TOK_TILE = 512
KEY_TILE = 512
SWA_TILE = 512
VMEM_LIMIT = 56 * 1024 * 1024
NEG_BIG = -1e30
F32_MAX = float(np.finfo(np.float32).max)
PHASE1_CAP = 48

ATT_SCALE = HEAD_DIM ** -0.5
IDX_SCALE = (N_IDX_HEADS ** -0.5) * (IDX_DIM ** -0.5)

G_QS, G_KS, G_QD, G_KD, G_QI, G_KI = 0, 4, 6, 8, 9, 11
N_ROPE_GROUPS = 12
G_VS, G_QM = 12, 14
N_RM_GROUPS = 16

WT_ROWS = 80

_NT = (((1,), (1,)), ((), ()))


def _cparams(n_grid):
    return pltpu.CompilerParams(
        dimension_semantics=("arbitrary",) * n_grid, vmem_limit_bytes=VMEM_LIMIT)


def _const_spec(shape):
    nd = len(shape)
    return pl.BlockSpec(shape, lambda *_: (0,) * nd, pipeline_mode=pl.Buffered(1))


def _rms_scale(x):
    return lax.rsqrt(jnp.mean(x * x, axis=-1, keepdims=True) + EPS)


def _first_half_mask():
    lane = lax.broadcasted_iota(jnp.int32, (1, LANES), 1)
    return (lane % HEAD_DIM) < (HEAD_DIM // 2), lane < HEAD_DIM


def _stack_heads(q, n_heads, low64):
    parts = []
    for h in range(n_heads):
        col = q[:, (h // 2) * LANES:(h // 2 + 1) * LANES]
        keep = low64 if h % 2 == 0 else jnp.logical_not(low64)
        parts.append(jnp.where(keep, col, jnp.zeros_like(col)))
    return jnp.concatenate(parts, axis=0)


def _memkv_kernel(mem_ref, g_ref, w_ref, k_ref, v_ref):
    m = mem_ref[0]
    mn = (m * _rms_scale(m) * g_ref[...]).astype(jnp.bfloat16)
    kv = jnp.dot(mn, w_ref[...], preferred_element_type=jnp.float32)
    k_ref[0] = kv[:, :MEM_WIDTH].astype(jnp.bfloat16)
    v_ref[0] = kv[:, MEM_WIDTH:].astype(jnp.bfloat16)


def _memkv(mem, g_mem, w_mem_kv):
    B, M, _ = mem.shape
    return pl.pallas_call(
        _memkv_kernel,
        grid=(B,),
        in_specs=[pl.BlockSpec((1, M, D_MODEL), lambda b: (b, 0, 0)),
                  _const_spec((1, D_MODEL)),
                  _const_spec((D_MODEL, 2 * MEM_WIDTH))],
        out_specs=[pl.BlockSpec((1, M, MEM_WIDTH), lambda b: (b, 0, 0)),
                   pl.BlockSpec((1, M, MEM_WIDTH), lambda b: (b, 0, 0))],
        out_shape=[jax.ShapeDtypeStruct((B, M, MEM_WIDTH), jnp.bfloat16)] * 2,
        compiler_params=_cparams(1),
        name="memkv",
    )(mem, g_mem.reshape(1, D_MODEL), w_mem_kv.astype(jnp.bfloat16))


def _proj_kernel(x_ref, pos_ref, g_ref, invf_ref, wrm_ref, wt_ref, wg_ref, bg_ref, mk_ref, mv_ref,
                 qs_ref, ks_ref, vs_ref, qd_ref, kd_ref, qi_ref, ki_ref, vdt_ref, wit_ref,
                 gates_ref, omem_ref):
    x = x_ref[0]
    h = (x * _rms_scale(x) * g_ref[...]).astype(jnp.bfloat16)
    half32, low64 = _first_half_mask()

    ang = pos_ref[0].astype(jnp.float32) * invf_ref[...]
    cos = jnp.cos(ang)
    sin = jnp.where(half32, -jnp.sin(ang), jnp.sin(ang))

    def rope(y):
        partner = jnp.where(half32, pltpu.roll(y, LANES - HEAD_DIM // 2, 1),
                            pltpu.roll(y, HEAD_DIM // 2, 1))
        return y * cos + partner * sin

    def group(g):
        return jnp.dot(h, wrm_ref[:, g * LANES:(g + 1) * LANES], preferred_element_type=jnp.float32)

    outs = {G_QS: (qs_ref, 4), G_KS: (ks_ref, 2), G_QD: (qd_ref, 2), G_KD: (kd_ref, 1),
            G_QI: (qi_ref, 2), G_KI: (ki_ref, 1), G_VS: (vs_ref, 2)}
    for g0, (ref, n) in outs.items():
        for i in range(n):
            y = group(g0 + i)
            if g0 + i < N_ROPE_GROUPS:
                y = rope(y)
            ref[0, :, i * LANES:(i + 1) * LANES] = y.astype(jnp.bfloat16)

    yt = lax.dot_general(wt_ref[...], h, _NT, preferred_element_type=jnp.float32)
    vdt_ref[0, 0] = yt[:HEAD_DIM].astype(jnp.bfloat16)
    wit_ref[0] = yt[HEAD_DIM:HEAD_DIM + 8]

    n_gc = (N_BRANCH * D_MODEL) // 512
    for c in range(n_gc):
        z = jnp.dot(h, wg_ref[:, c * 512:(c + 1) * 512], preferred_element_type=jnp.float32)
        z = z + bg_ref[:, c * 512:(c + 1) * 512]
        gates_ref[0, :, c * 512:(c + 1) * 512] = jax.nn.sigmoid(z).astype(jnp.bfloat16)

    for p in range(N_MEM_HEADS // 2):
        qm = group(G_QM + p).astype(jnp.bfloat16)
        mk = mk_ref[0, :, p * LANES:(p + 1) * LANES]
        mv = mv_ref[0, :, p * LANES:(p + 1) * LANES]
        o_pair = None
        for hh in range(2):
            keep = low64 if hh == 0 else jnp.logical_not(low64)
            qh = jnp.where(keep, qm, jnp.zeros_like(qm))
            s = lax.dot_general(qh, mk, _NT, preferred_element_type=jnp.float32)
            e = jnp.exp(s - jnp.max(s, axis=-1, keepdims=True))
            l = jnp.sum(e, axis=-1, keepdims=True)
            o = jnp.dot(e.astype(jnp.bfloat16), mv, preferred_element_type=jnp.float32) / l
            o_pair = o if hh == 0 else jnp.where(low64, o_pair, o)
        omem_ref[0, :, p * LANES:(p + 1) * LANES] = o_pair.astype(jnp.bfloat16)


def _proj(x, positions, g_mix, w_in, b_gate, mk, mv):
    B, S, _ = x.shape
    M = mk.shape[1]
    T = TOK_TILE
    assert S % T == 0
    sp = np.cumsum((0,) + IN_SPLITS)
    w = {n: w_in[:, sp[i]:sp[i + 1]] for i, n in enumerate(
        ("qs", "ks", "vs", "qd", "kd", "vd", "qi", "ki", "wi", "qm", "gate"))}

    def dup(a):
        return jnp.concatenate([a, a], axis=1)

    wrm = jnp.concatenate(
        [w["qs"] * ATT_SCALE,
         dup(w["ks"][:, :HEAD_DIM]), dup(w["ks"][:, HEAD_DIM:]),
         w["qd"] * ATT_SCALE, dup(w["kd"]),
         w["qi"], dup(w["ki"]),
         dup(w["vs"][:, :HEAD_DIM]), dup(w["vs"][:, HEAD_DIM:]),
         w["qm"] * ATT_SCALE], axis=1).astype(jnp.bfloat16)
    assert wrm.shape[1] == N_RM_GROUPS * LANES
    wt = jnp.concatenate(
        [w["vd"].T, w["wi"].T * IDX_SCALE, jnp.zeros((WT_ROWS - HEAD_DIM - N_IDX_HEADS, D_MODEL), w_in.dtype)],
        axis=0).astype(jnp.bfloat16)
    wg = w["gate"].astype(jnp.bfloat16)
    half = HEAD_DIM // 2
    invf = jnp.power(ROPE_THETA, -jnp.arange(half, dtype=jnp.float32) / half)
    invf = jnp.tile(invf, LANES // half).reshape(1, LANES)

    tok = lambda n: pl.BlockSpec((1, T, n), lambda b, i: (b, i, 0))
    bf = jnp.bfloat16
    out_shape = [
        jax.ShapeDtypeStruct((B, S, 4 * LANES), bf),
        jax.ShapeDtypeStruct((B, S, 2 * LANES), bf),
        jax.ShapeDtypeStruct((B, S, 2 * LANES), bf),
        jax.ShapeDtypeStruct((B, S, 2 * LANES), bf),
        jax.ShapeDtypeStruct((B, S, LANES), bf),
        jax.ShapeDtypeStruct((B, S, 2 * LANES), bf),
        jax.ShapeDtypeStruct((B, S, LANES), bf),
        jax.ShapeDtypeStruct((B, S // T, HEAD_DIM, T), bf),
        jax.ShapeDtypeStruct((B, 8, S), jnp.float32),
        jax.ShapeDtypeStruct((B, S, N_BRANCH * D_MODEL), bf),
        jax.ShapeDtypeStruct((B, S, MEM_WIDTH), bf),
    ]
    out_specs = [tok(4 * LANES), tok(2 * LANES), tok(2 * LANES), tok(2 * LANES), tok(LANES),
                 tok(2 * LANES), tok(LANES),
                 pl.BlockSpec((1, 1, HEAD_DIM, T), lambda b, i: (b, i, 0, 0)),
                 pl.BlockSpec((1, 8, T), lambda b, i: (b, 0, i)),
                 tok(N_BRANCH * D_MODEL), tok(MEM_WIDTH)]
    return pl.pallas_call(
        _proj_kernel,
        grid=(B, S // T),
        in_specs=[tok(D_MODEL),
                  pl.BlockSpec((1, T, 1), lambda b, i: (b, i, 0)),
                  _const_spec((1, D_MODEL)), _const_spec((1, LANES)),
                  _const_spec(wrm.shape), _const_spec(wt.shape), _const_spec(wg.shape),
                  _const_spec((1, N_BRANCH * D_MODEL)),
                  pl.BlockSpec((1, M, MEM_WIDTH), lambda b, i: (b, 0, 0)),
                  pl.BlockSpec((1, M, MEM_WIDTH), lambda b, i: (b, 0, 0))],
        out_specs=out_specs,
        out_shape=out_shape,
        compiler_params=_cparams(2),
        name="proj",
    )(x, positions.reshape(B, S, 1), g_mix.reshape(1, D_MODEL), invf, wrm, wt, wg,
      b_gate.reshape(1, -1), mk, mv)


def _swa_kernel(sink_ref, q_ref, kc_ref, kp_ref, vc_ref, vp_ref, o_ref):
    i = pl.program_id(1)
    _, low64 = _first_half_mask()
    q = q_ref[0]
    kcat = jnp.concatenate([kp_ref[0], kc_ref[0]], axis=0)
    vcat = jnp.concatenate([vp_ref[0], vc_ref[0]], axis=0)
    qi = lax.broadcasted_iota(jnp.int32, (BLOCK, 2 * BLOCK), 0)
    sj = lax.broadcasted_iota(jnp.int32, (BLOCK, 2 * BLOCK), 1)
    dist = qi + BLOCK - sj
    band = (dist >= 0) & (dist < WINDOW)
    G = N_SWA_HEADS // N_SWA_KV
    for bq in range(SWA_TILE // BLOCK):
        valid = band
        if bq == 0:
            valid = band & ((sj + jnp.where(i > 0, BLOCK, 0)) >= BLOCK)
        qb = q[bq * BLOCK:(bq + 1) * BLOCK]
        for g in range(N_SWA_KV):
            qst = _stack_heads(qb[:, g * G * HEAD_DIM:(g + 1) * G * HEAD_DIM], G, low64)
            kw = kcat[bq * BLOCK:(bq + 2) * BLOCK, g * LANES:(g + 1) * LANES]
            vw = vcat[bq * BLOCK:(bq + 2) * BLOCK, g * LANES:(g + 1) * LANES]
            s = lax.dot_general(qst, kw, _NT, preferred_element_type=jnp.float32)
            es, ls = [], []
            for hh in range(G):
                sh = jnp.where(valid, s[hh * BLOCK:(hh + 1) * BLOCK], -jnp.inf)
                sink = sink_ref[g * G + hh]
                m = jnp.maximum(jnp.max(sh, axis=-1, keepdims=True), sink)
                e = jnp.exp(sh - m)
                ls.append(jnp.sum(e, axis=-1, keepdims=True) + jnp.exp(sink - m))
                es.append(e.astype(jnp.bfloat16))
            o = jnp.dot(jnp.concatenate(es, axis=0), vw, preferred_element_type=jnp.float32)
            for pp in range(G // 2):
                o0 = o[(2 * pp) * BLOCK:(2 * pp + 1) * BLOCK] / ls[2 * pp]
                o1 = o[(2 * pp + 1) * BLOCK:(2 * pp + 2) * BLOCK] / ls[2 * pp + 1]
                c0 = (g * (G // 2) + pp) * LANES
                o_ref[0, bq * BLOCK:(bq + 1) * BLOCK, c0:c0 + LANES] = (
                    jnp.where(low64, o0, o1).astype(jnp.bfloat16))


def _swa(qs, ks, vs, sinks):
    B, S, _ = qs.shape
    TQ = SWA_TILE
    r = TQ // BLOCK
    cur = lambda n: pl.BlockSpec((1, TQ, n), lambda b, i: (b, i, 0))
    prev = lambda n: pl.BlockSpec((1, BLOCK, n), lambda b, i: (b, jnp.maximum(i * r - 1, 0), 0))
    return pl.pallas_call(
        _swa_kernel,
        grid=(B, S // TQ),
        in_specs=[pl.BlockSpec(memory_space=pltpu.SMEM),
                  cur(4 * LANES), cur(2 * LANES), prev(2 * LANES), cur(2 * LANES), prev(2 * LANES)],
        out_specs=cur(SWA_WIDTH),
        out_shape=jax.ShapeDtypeStruct((B, S, SWA_WIDTH), jnp.bfloat16),
        compiler_params=_cparams(2),
        name="swa",
    )(sinks, qs, ks, ks, vs, vs)


def _dsa_kernel(qi_ref, wi_ref, qd_ref, ki_ref, kd_ref, vt_ref, o_ref, sc_ref, *, top_k, seq_len):
    j = pl.program_id(1)
    nt = j // (KEY_TILE // BLOCK) + 1
    _, low64 = _first_half_mask()
    lane = lax.broadcasted_iota(jnp.int32, (1, LANES), 1)
    qpos = j * BLOCK + lane
    w = wi_ref[0]
    q_idx = _stack_heads(qi_ref[0], N_IDX_HEADS, low64)
    q_att = _stack_heads(qd_ref[0], N_DSA_HEADS, low64)
    kf = jnp.float32(top_k)
    row = lax.broadcasted_iota(jnp.int32, (KEY_TILE, LANES), 0)
    n_acc = 64

    def fold(a, op):
        a = a.reshape(KEY_TILE // n_acc, n_acc, LANES)
        r = a[0]
        for u in range(1, KEY_TILE // n_acc):
            r = op(r, a[u])
        return r

    def idx_scores(t):
        ks = ki_ref[0, pl.ds(t * KEY_TILE, KEY_TILE), :]
        st = lax.dot_general(ks, q_idx, _NT, preferred_element_type=jnp.float32)
        isc = None
        for h in range(N_IDX_HEADS):
            term = jnp.maximum(st[:, h * LANES:(h + 1) * LANES], 0.0) * w[h:h + 1, :]
            isc = term if isc is None else isc + term
        return isc

    def idx_body(t, carry):
        mn, mx = carry
        isc = idx_scores(t)
        sc_ref[pl.ds(t * KEY_TILE, KEY_TILE), :] = isc
        return jnp.minimum(mn, fold(isc, jnp.minimum)), jnp.maximum(mx, fold(isc, jnp.maximum))

    mn0 = jnp.full((n_acc, LANES), jnp.inf, jnp.float32)
    mn, mx = lax.fori_loop(0, nt - 1, idx_body, (mn0, -mn0))
    t_last = nt - 1
    isc = idx_scores(t_last)
    valid = (row + t_last * KEY_TILE) <= qpos
    sc_ref[pl.ds(t_last * KEY_TILE, KEY_TILE), :] = jnp.where(valid, isc, -jnp.inf)
    mn = jnp.minimum(mn, fold(jnp.where(valid, isc, jnp.inf), jnp.minimum))
    mx = jnp.maximum(mx, fold(jnp.where(valid, isc, -jnp.inf), jnp.maximum))
    smin = jnp.min(mn, axis=0, keepdims=True)
    smax = jnp.max(mx, axis=0, keepdims=True)

    def count_ge(thr):
        def body(t, acc):
            blk = sc_ref[pl.ds(t * KEY_TILE, KEY_TILE), :]
            return acc + fold((blk >= thr).astype(jnp.float32), jnp.add)
        acc = lax.fori_loop(0, nt, body, jnp.zeros((n_acc, LANES), jnp.float32))
        return jnp.sum(acc, axis=0, keepdims=True)

    def max_below(bound):
        def body(t, acc):
            blk = sc_ref[pl.ds(t * KEY_TILE, KEY_TILE), :]
            return jnp.maximum(acc, fold(jnp.where(blk < bound, blk, -jnp.inf), jnp.maximum))
        acc = lax.fori_loop(0, nt, body, jnp.full((n_acc, LANES), -jnp.inf, jnp.float32))
        return jnp.max(acc, axis=0, keepdims=True)

    def count_tie_prefix(tval, jmax):
        def body(t, acc):
            blk = sc_ref[pl.ds(t * KEY_TILE, KEY_TILE), :]
            kidx = (row + t * KEY_TILE).astype(jnp.float32)
            hit = (blk == tval) & (kidx <= jmax)
            return acc + fold(hit.astype(jnp.float32), jnp.add)
        acc = lax.fori_loop(0, nt, body, jnp.zeros((n_acc, LANES), jnp.float32))
        return jnp.sum(acc, axis=0, keepdims=True)

    def all_set(flag):
        return jnp.min(flag) > 0.5

    def select_threshold():
        one = jnp.ones((1, LANES), jnp.float32)
        zero = jnp.zeros((1, LANES), jnp.float32)

        def p1_cond(st):
            it, lo, bhi, hx, chx, found, thr = st
            return (it < PHASE1_CAP) & jnp.logical_not(all_set(found))

        def p1_body(st):
            it, lo, bhi, hx, chx, found, thr = st
            mid = lo + (bhi - lo) * 0.5
            c = count_ge(mid)
            hit = (c == kf) & (found < 0.5)
            thr = jnp.where(hit, mid, thr)
            found = jnp.where(hit, one, found)
            ge = c >= kf
            lo = jnp.where(ge, mid, lo)
            bhi = jnp.where(ge, bhi, mid)
            hx = jnp.where(ge, hx, mid)
            chx = jnp.where(ge, chx, c)
            return it + 1, lo, bhi, hx, chx, found, thr

        st = (jnp.int32(0), smin, smax, jnp.full((1, LANES), jnp.inf, jnp.float32), zero, zero, smin)
        _, _, _, hx, chx, found, thr = lax.while_loop(p1_cond, p1_body, st)

        def p2_cond(st):
            return jnp.logical_not(all_set(st[2]))

        def p2_body(st):
            hx, chx, found, thr, extra = st
            u = max_below(hx)
            cu = count_ge(u)
            ok = (cu >= kf) & (found < 0.5)
            thr = jnp.where(ok, u, thr)
            extra = jnp.where(ok, cu - kf, extra)
            found = jnp.where(ok, one, found)
            go = found < 0.5
            hx = jnp.where(go, u, hx)
            chx = jnp.where(go, cu, chx)
            return hx, chx, found, thr, extra

        hx, chx, found, thr, extra = lax.while_loop(p2_cond, p2_body, (hx, chx, found, thr, zero))

        @pl.when(jnp.max(extra) > 0.5)
        def _():
            need = kf - chx
            n_it = int(np.ceil(np.log2(seq_len))) + 1

            def jb(_, st):
                lo_i, hi_i = st
                mid = jnp.floor((lo_i + hi_i) * 0.5)
                ge = count_tie_prefix(thr, mid) >= need
                return jnp.where(ge, lo_i, mid), jnp.where(ge, mid, hi_i)

            _, jcut = lax.fori_loop(0, n_it, jb, (-one, jnp.full((1, LANES), seq_len - 1.0, jnp.float32)))
            jcut = jnp.where(extra > 0.5, jcut, jnp.float32(seq_len))

            def drop(t, c):
                blk = sc_ref[pl.ds(t * KEY_TILE, KEY_TILE), :]
                kidx = (row + t * KEY_TILE).astype(jnp.float32)
                sc_ref[pl.ds(t * KEY_TILE, KEY_TILE), :] = jnp.where(
                    (blk == thr) & (kidx > jcut), -jnp.inf, blk)
                return c

            lax.fori_loop(0, nt, drop, 0)

        return thr

    thr = lax.cond((j + 1) * BLOCK > top_k, select_threshold,
                   lambda: jnp.full((1, LANES), -F32_MAX, jnp.float32))

    NH = N_DSA_HEADS

    def att_body(t, carry):
        m, l, acc = carry
        kd = kd_ref[0, pl.ds(t * KEY_TILE, KEY_TILE), :]
        sd = lax.dot_general(kd, q_att, _NT, preferred_element_type=jnp.float32)
        sel = sc_ref[pl.ds(t * KEY_TILE, KEY_TILE), :] >= thr
        sd = jnp.concatenate(
            [jnp.where(sel, sd[:, h * LANES:(h + 1) * LANES], -jnp.inf) for h in range(NH)], axis=1)
        m_new = jnp.maximum(m, jnp.max(sd, axis=0, keepdims=True))
        alpha = jnp.exp(m - m_new)
        p = jnp.exp(sd - m_new)
        l = l * alpha + jnp.sum(p, axis=0, keepdims=True)
        pv = jnp.dot(vt_ref[0, t], p.astype(jnp.bfloat16), preferred_element_type=jnp.float32)
        return m_new, l, acc * alpha + pv

    m0 = jnp.full((1, NH * LANES), NEG_BIG, jnp.float32)
    l0 = jnp.zeros((1, NH * LANES), jnp.float32)
    a0 = jnp.zeros((HEAD_DIM, NH * LANES), jnp.float32)
    _, l, acc = lax.fori_loop(0, nt, att_body, (m0, l0, a0))
    ot = acc / l
    for p in range(NH // 2):
        pair = jnp.concatenate([ot[:, (2 * p) * LANES:(2 * p + 1) * LANES],
                                ot[:, (2 * p + 1) * LANES:(2 * p + 2) * LANES]], axis=0)
        o_ref[0, :, p * LANES:(p + 1) * LANES] = pair.T.astype(jnp.bfloat16)


def _dsa(qi, wit, qd, ki, kd, vdt, top_k):
    B, S, _ = qi.shape
    assert S % KEY_TILE == 0 and KEY_TILE % BLOCK == 0
    blk = lambda n: pl.BlockSpec((1, BLOCK, n), lambda b, j: (b, j, 0))
    whole = lambda n: pl.BlockSpec((1, S, n), lambda b, j: (b, 0, 0))
    return pl.pallas_call(
        functools.partial(_dsa_kernel, top_k=top_k, seq_len=S),
        grid=(B, S // BLOCK),
        in_specs=[blk(2 * LANES),
                  pl.BlockSpec((1, 8, BLOCK), lambda b, j: (b, 0, j)),
                  blk(2 * LANES), whole(LANES), whole(LANES),
                  pl.BlockSpec((1, S // KEY_TILE, HEAD_DIM, KEY_TILE), lambda b, j: (b, 0, 0, 0))],
        out_specs=blk(DSA_WIDTH),
        out_shape=jax.ShapeDtypeStruct((B, S, DSA_WIDTH), jnp.bfloat16),
        scratch_shapes=[pltpu.VMEM((S, LANES), jnp.float32)],
        compiler_params=_cparams(2),
        name="dsa",
    )(qi, wit, qd, ki, kd, vdt)


def _merge_kernel(x_ref, gates_ref, os_ref, od_ref, om_ref, wps_ref, wpd_ref, wpm_ref, wo_ref,
                  gm_ref, w1_ref, w2_ref, gf_ref, out_ref):
    f32 = jnp.float32
    x = x_ref[...]
    merged = None
    for b, (o_ref, w_ref) in enumerate(((os_ref, wps_ref), (od_ref, wpd_ref), (om_ref, wpm_ref))):
        y = jnp.dot(o_ref[...], w_ref[...], preferred_element_type=f32)
        y = y * gates_ref[:, b * D_MODEL:(b + 1) * D_MODEL].astype(f32)
        merged = y if merged is None else merged + y
    x1 = x + jnp.dot(merged.astype(jnp.bfloat16), wo_ref[...], preferred_element_type=f32)
    h = (x1 * _rms_scale(x1) * gm_ref[...]).astype(jnp.bfloat16)
    acc = x1
    FC = 1024
    for c in range(D_FF // FC):
        hid = jnp.maximum(jnp.dot(h, w1_ref[:, c * FC:(c + 1) * FC], preferred_element_type=f32), 0.0)
        hid = (hid * hid).astype(jnp.bfloat16)
        acc = acc + jnp.dot(hid, w2_ref[c * FC:(c + 1) * FC, :], preferred_element_type=f32)
    out_ref[...] = acc * _rms_scale(acc) * gf_ref[...]


def _merge(x2d, gates, o_swa, o_dsa, o_mem, w_proj_swa, w_proj_dsa, w_proj_mem, w_out,
           g_mlp, w_mlp_in, w_mlp_out, g_final):
    N = x2d.shape[0]
    T = TOK_TILE
    bf = jnp.bfloat16
    tok = lambda n: pl.BlockSpec((T, n), lambda i: (i, 0))
    ws = [w_proj_swa.astype(bf), w_proj_dsa.astype(bf), w_proj_mem.astype(bf), w_out.astype(bf)]
    w1, w2 = w_mlp_in.astype(bf), w_mlp_out.astype(bf)
    return pl.pallas_call(
        _merge_kernel,
        grid=(N // T,),
        in_specs=[tok(D_MODEL), tok(N_BRANCH * D_MODEL), tok(SWA_WIDTH), tok(DSA_WIDTH), tok(MEM_WIDTH)]
                 + [_const_spec(a.shape) for a in ws]
                 + [_const_spec((1, D_MODEL)), _const_spec(w1.shape), _const_spec(w2.shape),
                    _const_spec((1, D_MODEL))],
        out_specs=tok(D_MODEL),
        out_shape=jax.ShapeDtypeStruct((N, D_MODEL), jnp.float32),
        compiler_params=_cparams(1),
        name="merge",
    )(x2d, gates, o_swa, o_dsa, o_mem, *ws, g_mlp.reshape(1, D_MODEL), w1, w2,
      g_final.reshape(1, D_MODEL))


def kernel(x, mem, positions, g_mix, w_in, b_gate, sinks, g_mem, w_mem_kv, w_proj_swa, w_proj_dsa,
           w_proj_mem, w_out, g_mlp, w_mlp_in, w_mlp_out, g_final):
    B, S, D = x.shape
    assert g_mix.shape[0] == 1, "the final norm is fused into the single layer's merge kernel"
    top_k = min(TOPK_MAX, S // 4)
    for l in range(1):
        mk, mv = _memkv(mem, g_mem[l], w_mem_kv[l])
        (qs, ks, vs, qd, kd, qi, ki, vdt, wit, gates, o_mem) = _proj(
            x, positions, g_mix[l], w_in[l], b_gate[l], mk, mv)
        o_swa = _swa(qs, ks, vs, sinks[l])
        o_dsa = _dsa(qi, wit, qd, ki, kd, vdt, top_k)
        x = _merge(x.reshape(B * S, D), gates.reshape(B * S, -1), o_swa.reshape(B * S, -1),
                   o_dsa.reshape(B * S, -1), o_mem.reshape(B * S, -1), w_proj_swa[l], w_proj_dsa[l],
                   w_proj_mem[l], w_out[l], g_mlp[l], w_mlp_in[l], w_mlp_out[l], g_final).reshape(B, S, D)
    return x
```

```python
import functools

import jax
import jax.numpy as jnp
import numpy as np
from jax import lax
from jax.experimental import pallas as pl
from jax.experimental.pallas import tpu as pltpu

D_MODEL = 1024
HEAD_DIM = 64
N_SWA_HEADS = 8
N_SWA_KV = 2
WINDOW = 128
BLOCK = 128
N_DSA_HEADS = 4
N_IDX_HEADS = 4
IDX_DIM = 64
TOPK_MAX = 256
N_MEM_HEADS = 4
N_BRANCH = 3
D_FF = 4 * D_MODEL
ROPE_THETA = 10000.0
EPS = 1e-6

SWA_WIDTH = N_SWA_HEADS * HEAD_DIM
SWA_KV_WIDTH = N_SWA_KV * HEAD_DIM
DSA_WIDTH = N_DSA_HEADS * HEAD_DIM
MEM_WIDTH = N_MEM_HEADS * HEAD_DIM
IN_SPLITS = (SWA_WIDTH, SWA_KV_WIDTH, SWA_KV_WIDTH,
             DSA_WIDTH, HEAD_DIM, HEAD_DIM,
             N_IDX_HEADS * IDX_DIM, IDX_DIM, N_IDX_HEADS,
             MEM_WIDTH, N_BRANCH * D_MODEL)

LANES = 128
TOK_TILE = 512
KEY_TILE = 512
SWA_TILE = 512
VMEM_LIMIT = 56 * 1024 * 1024
NEG_BIG = -1e30
F32_MAX = float(np.finfo(np.float32).max)
N_INTERP = 6
TOP_R = 4
AIM_OFFSET = -2.5

ATT_SCALE = HEAD_DIM ** -0.5
IDX_SCALE = (N_IDX_HEADS ** -0.5) * (IDX_DIM ** -0.5)

G_QS, G_KS, G_QD, G_KD, G_QI, G_KI = 0, 4, 6, 8, 9, 11
N_ROPE_GROUPS = 12
G_VS, G_QM = 12, 14
N_RM_GROUPS = 16

WT_ROWS = 80

_NT = (((1,), (1,)), ((), ()))


def _cparams(n_grid):
    return pltpu.CompilerParams(
        dimension_semantics=("arbitrary",) * n_grid, vmem_limit_bytes=VMEM_LIMIT)


def _const_spec(shape):
    nd = len(shape)
    return pl.BlockSpec(shape, lambda *_: (0,) * nd, pipeline_mode=pl.Buffered(1))


def _rms_scale(x):
    return lax.rsqrt(jnp.mean(x * x, axis=-1, keepdims=True) + EPS)


def _first_half_mask():
    lane = lax.broadcasted_iota(jnp.int32, (1, LANES), 1)
    return (lane % HEAD_DIM) < (HEAD_DIM // 2), lane < HEAD_DIM


def _stack_heads(q, n_heads, low64):
    parts = []
    for h in range(n_heads):
        col = q[:, (h // 2) * LANES:(h // 2 + 1) * LANES]
        keep = low64 if h % 2 == 0 else jnp.logical_not(low64)
        parts.append(jnp.where(keep, col, jnp.zeros_like(col)))
    return jnp.concatenate(parts, axis=0)


def _memkv_kernel(mem_ref, g_ref, w_ref, k_ref, v_ref):
    m = mem_ref[0]
    mn = (m * _rms_scale(m) * g_ref[...]).astype(jnp.bfloat16)
    kv = jnp.dot(mn, w_ref[...], preferred_element_type=jnp.float32)
    k_ref[0] = kv[:, :MEM_WIDTH].astype(jnp.bfloat16)
    v_ref[0] = kv[:, MEM_WIDTH:].astype(jnp.bfloat16)


def _memkv(mem, g_mem, w_mem_kv):
    B, M, _ = mem.shape
    return pl.pallas_call(
        _memkv_kernel,
        grid=(B,),
        in_specs=[pl.BlockSpec((1, M, D_MODEL), lambda b: (b, 0, 0)),
                  _const_spec((1, D_MODEL)),
                  _const_spec((D_MODEL, 2 * MEM_WIDTH))],
        out_specs=[pl.BlockSpec((1, M, MEM_WIDTH), lambda b: (b, 0, 0)),
                   pl.BlockSpec((1, M, MEM_WIDTH), lambda b: (b, 0, 0))],
        out_shape=[jax.ShapeDtypeStruct((B, M, MEM_WIDTH), jnp.bfloat16)] * 2,
        compiler_params=_cparams(1),
        name="memkv",
    )(mem, g_mem.reshape(1, D_MODEL), w_mem_kv.astype(jnp.bfloat16))


def _proj_kernel(x_ref, pos_ref, g_ref, invf_ref, wrm_ref, wt_ref, wg_ref, bg_ref, mk_ref, mv_ref,
                 qs_ref, ks_ref, vs_ref, qd_ref, kd_ref, qi_ref, ki_ref, vdt_ref, wit_ref,
                 gates_ref, omem_ref):
    x = x_ref[0]
    h = (x * _rms_scale(x) * g_ref[...]).astype(jnp.bfloat16)
    half32, low64 = _first_half_mask()

    ang = pos_ref[0].astype(jnp.float32) * invf_ref[...]
    cos = jnp.cos(ang)
    sin = jnp.where(half32, -jnp.sin(ang), jnp.sin(ang))

    def rope(y):
        partner = jnp.where(half32, pltpu.roll(y, LANES - HEAD_DIM // 2, 1),
                            pltpu.roll(y, HEAD_DIM // 2, 1))
        return y * cos + partner * sin

    def group(g):
        return jnp.dot(h, wrm_ref[:, g * LANES:(g + 1) * LANES], preferred_element_type=jnp.float32)

    outs = {G_QS: (qs_ref, 4), G_KS: (ks_ref, 2), G_QD: (qd_ref, 2), G_KD: (kd_ref, 1),
            G_QI: (qi_ref, 2), G_KI: (ki_ref, 1), G_VS: (vs_ref, 2)}
    for g0, (ref, n) in outs.items():
        for i in range(n):
            y = group(g0 + i)
            if g0 + i < N_ROPE_GROUPS:
                y = rope(y)
            ref[0, :, i * LANES:(i + 1) * LANES] = y.astype(jnp.bfloat16)

    yt = lax.dot_general(wt_ref[...], h, _NT, preferred_element_type=jnp.float32)
    vdt_ref[0, 0] = yt[:HEAD_DIM].astype(jnp.bfloat16)
    wit_ref[0] = yt[HEAD_DIM:HEAD_DIM + 8]

    n_gc = (N_BRANCH * D_MODEL) // 512
    for c in range(n_gc):
        z = jnp.dot(h, wg_ref[:, c * 512:(c + 1) * 512], preferred_element_type=jnp.float32)
        z = z + bg_ref[:, c * 512:(c + 1) * 512]
        gates_ref[0, :, c * 512:(c + 1) * 512] = jax.nn.sigmoid(z).astype(jnp.bfloat16)

    for p in range(N_MEM_HEADS // 2):
        qm = group(G_QM + p).astype(jnp.bfloat16)
        mk = mk_ref[0, :, p * LANES:(p + 1) * LANES]
        mv = mv_ref[0, :, p * LANES:(p + 1) * LANES]
        o_pair = None
        for hh in range(2):
            keep = low64 if hh == 0 else jnp.logical_not(low64)
            qh = jnp.where(keep, qm, jnp.zeros_like(qm))
            s = lax.dot_general(qh, mk, _NT, preferred_element_type=jnp.float32)
            e = jnp.exp(s - jnp.max(s, axis=-1, keepdims=True))
            l = jnp.sum(e, axis=-1, keepdims=True)
            o = jnp.dot(e.astype(jnp.bfloat16), mv, preferred_element_type=jnp.float32) / l
            o_pair = o if hh == 0 else jnp.where(low64, o_pair, o)
        omem_ref[0, :, p * LANES:(p + 1) * LANES] = o_pair.astype(jnp.bfloat16)


def _proj(x, positions, g_mix, w_in, b_gate, mk, mv):
    B, S, _ = x.shape
    M = mk.shape[1]
    T = TOK_TILE
    assert S % T == 0
    sp = np.cumsum((0,) + IN_SPLITS)
    w = {n: w_in[:, sp[i]:sp[i + 1]] for i, n in enumerate(
        ("qs", "ks", "vs", "qd", "kd", "vd", "qi", "ki", "wi", "qm", "gate"))}

    def dup(a):
        return jnp.concatenate([a, a], axis=1)

    wrm = jnp.concatenate(
        [w["qs"] * ATT_SCALE,
         dup(w["ks"][:, :HEAD_DIM]), dup(w["ks"][:, HEAD_DIM:]),
         w["qd"] * ATT_SCALE, dup(w["kd"]),
         w["qi"], dup(w["ki"]),
         dup(w["vs"][:, :HEAD_DIM]), dup(w["vs"][:, HEAD_DIM:]),
         w["qm"] * ATT_SCALE], axis=1).astype(jnp.bfloat16)
    assert wrm.shape[1] == N_RM_GROUPS * LANES
    wt = jnp.concatenate(
        [w["vd"].T, w["wi"].T * IDX_SCALE, jnp.zeros((WT_ROWS - HEAD_DIM - N_IDX_HEADS, D_MODEL), w_in.dtype)],
        axis=0).astype(jnp.bfloat16)
    wg = w["gate"].astype(jnp.bfloat16)
    half = HEAD_DIM // 2
    invf = jnp.power(ROPE_THETA, -jnp.arange(half, dtype=jnp.float32) / half)
    invf = jnp.tile(invf, LANES // half).reshape(1, LANES)

    tok = lambda n: pl.BlockSpec((1, T, n), lambda b, i: (b, i, 0))
    bf = jnp.bfloat16
    out_shape = [
        jax.ShapeDtypeStruct((B, S, 4 * LANES), bf),
        jax.ShapeDtypeStruct((B, S, 2 * LANES), bf),
        jax.ShapeDtypeStruct((B, S, 2 * LANES), bf),
        jax.ShapeDtypeStruct((B, S, 2 * LANES), bf),
        jax.ShapeDtypeStruct((B, S, LANES), bf),
        jax.ShapeDtypeStruct((B, S, 2 * LANES), bf),
        jax.ShapeDtypeStruct((B, S, LANES), bf),
        jax.ShapeDtypeStruct((B, S // T, HEAD_DIM, T), bf),
        jax.ShapeDtypeStruct((B, 8, S), jnp.float32),
        jax.ShapeDtypeStruct((B, S, N_BRANCH * D_MODEL), bf),
        jax.ShapeDtypeStruct((B, S, MEM_WIDTH), bf),
    ]
    out_specs = [tok(4 * LANES), tok(2 * LANES), tok(2 * LANES), tok(2 * LANES), tok(LANES),
                 tok(2 * LANES), tok(LANES),
                 pl.BlockSpec((1, 1, HEAD_DIM, T), lambda b, i: (b, i, 0, 0)),
                 pl.BlockSpec((1, 8, T), lambda b, i: (b, 0, i)),
                 tok(N_BRANCH * D_MODEL), tok(MEM_WIDTH)]
    return pl.pallas_call(
        _proj_kernel,
        grid=(B, S // T),
        in_specs=[tok(D_MODEL),
                  pl.BlockSpec((1, T, 1), lambda b, i: (b, i, 0)),
                  _const_spec((1, D_MODEL)), _const_spec((1, LANES)),
                  _const_spec(wrm.shape), _const_spec(wt.shape), _const_spec(wg.shape),
                  _const_spec((1, N_BRANCH * D_MODEL)),
                  pl.BlockSpec((1, M, MEM_WIDTH), lambda b, i: (b, 0, 0)),
                  pl.BlockSpec((1, M, MEM_WIDTH), lambda b, i: (b, 0, 0))],
        out_specs=out_specs,
        out_shape=out_shape,
        compiler_params=_cparams(2),
        name="proj",
    )(x, positions.reshape(B, S, 1), g_mix.reshape(1, D_MODEL), invf, wrm, wt, wg,
      b_gate.reshape(1, -1), mk, mv)


def _swa_kernel(sink_ref, q_ref, kc_ref, kp_ref, vc_ref, vp_ref, o_ref):
    i = pl.program_id(1)
    _, low64 = _first_half_mask()
    q = q_ref[0]
    kcat = jnp.concatenate([kp_ref[0], kc_ref[0]], axis=0)
    vcat = jnp.concatenate([vp_ref[0], vc_ref[0]], axis=0)
    qi = lax.broadcasted_iota(jnp.int32, (BLOCK, 2 * BLOCK), 0)
    sj = lax.broadcasted_iota(jnp.int32, (BLOCK, 2 * BLOCK), 1)
    dist = qi + BLOCK - sj
    band = (dist >= 0) & (dist < WINDOW)
    G = N_SWA_HEADS // N_SWA_KV
    for bq in range(SWA_TILE // BLOCK):
        valid = band
        if bq == 0:
            valid = band & ((sj + jnp.where(i > 0, BLOCK, 0)) >= BLOCK)
        qb = q[bq * BLOCK:(bq + 1) * BLOCK]
        for g in range(N_SWA_KV):
            qst = _stack_heads(qb[:, g * G * HEAD_DIM:(g + 1) * G * HEAD_DIM], G, low64)
            kw = kcat[bq * BLOCK:(bq + 2) * BLOCK, g * LANES:(g + 1) * LANES]
            vw = vcat[bq * BLOCK:(bq + 2) * BLOCK, g * LANES:(g + 1) * LANES]
            s = lax.dot_general(qst, kw, _NT, preferred_element_type=jnp.float32)
            es, ls = [], []
            for hh in range(G):
                sh = jnp.where(valid, s[hh * BLOCK:(hh + 1) * BLOCK], -jnp.inf)
                sink = sink_ref[g * G + hh]
                m = jnp.maximum(jnp.max(sh, axis=-1, keepdims=True), sink)
                e = jnp.exp(sh - m)
                ls.append(jnp.sum(e, axis=-1, keepdims=True) + jnp.exp(sink - m))
                es.append(e.astype(jnp.bfloat16))
            o = jnp.dot(jnp.concatenate(es, axis=0), vw, preferred_element_type=jnp.float32)
            for pp in range(G // 2):
                o0 = o[(2 * pp) * BLOCK:(2 * pp + 1) * BLOCK] / ls[2 * pp]
                o1 = o[(2 * pp + 1) * BLOCK:(2 * pp + 2) * BLOCK] / ls[2 * pp + 1]
                c0 = (g * (G // 2) + pp) * LANES
                o_ref[0, bq * BLOCK:(bq + 1) * BLOCK, c0:c0 + LANES] = (
                    jnp.where(low64, o0, o1).astype(jnp.bfloat16))


def _swa(qs, ks, vs, sinks):
    B, S, _ = qs.shape
    TQ = SWA_TILE
    r = TQ // BLOCK
    cur = lambda n: pl.BlockSpec((1, TQ, n), lambda b, i: (b, i, 0))
    prev = lambda n: pl.BlockSpec((1, BLOCK, n), lambda b, i: (b, jnp.maximum(i * r - 1, 0), 0))
    return pl.pallas_call(
        _swa_kernel,
        grid=(B, S // TQ),
        in_specs=[pl.BlockSpec(memory_space=pltpu.SMEM),
                  cur(4 * LANES), cur(2 * LANES), prev(2 * LANES), cur(2 * LANES), prev(2 * LANES)],
        out_specs=cur(SWA_WIDTH),
        out_shape=jax.ShapeDtypeStruct((B, S, SWA_WIDTH), jnp.bfloat16),
        compiler_params=_cparams(2),
        name="swa",
    )(sinks, qs, ks, ks, vs, vs)


def _dsa_kernel(qi_ref, wi_ref, qd_ref, ki_ref, kd_ref, vt_ref, ltri_ref, o_ref, sc_ref, *, top_k):
    j = pl.program_id(1)
    nt = j // (KEY_TILE // BLOCK) + 1
    _, low64 = _first_half_mask()
    lane = lax.broadcasted_iota(jnp.int32, (1, LANES), 1)
    qpos = j * BLOCK + lane
    w = wi_ref[0]
    q_idx = _stack_heads(qi_ref[0], N_IDX_HEADS, low64)
    q_att = _stack_heads(qd_ref[0], N_DSA_HEADS, low64)
    kf = jnp.float32(top_k)
    row = lax.broadcasted_iota(jnp.int32, (KEY_TILE, LANES), 0)
    n_acc = 64

    def fold(a, op):
        a = a.reshape(KEY_TILE // n_acc, n_acc, LANES)
        r = a[0]
        for u in range(1, KEY_TILE // n_acc):
            r = op(r, a[u])
        return r

    def idx_scores(t):
        ks = ki_ref[0, pl.ds(t * KEY_TILE, KEY_TILE), :]
        st = lax.dot_general(ks, q_idx, _NT, preferred_element_type=jnp.float32)
        isc = None
        for h in range(N_IDX_HEADS):
            term = jnp.maximum(st[:, h * LANES:(h + 1) * LANES], 0.0) * w[h:h + 1, :]
            isc = term if isc is None else isc + term
        return isc

    def idx_stats(carry, lo_src, isc):
        mn, mx, cge, cgt = carry
        return (jnp.minimum(mn, fold(lo_src, jnp.minimum)),
                jnp.maximum(mx, fold(isc, jnp.maximum)),
                cge + fold((isc >= 0.0).astype(jnp.float32), jnp.add),
                cgt + fold((isc > 0.0).astype(jnp.float32), jnp.add))

    def idx_body(t, carry):
        isc = idx_scores(t)
        sc_ref[pl.ds(t * KEY_TILE, KEY_TILE), :] = isc
        return idx_stats(carry, isc, isc)

    inf_acc = jnp.full((n_acc, LANES), jnp.inf, jnp.float32)
    zero_acc = jnp.zeros((n_acc, LANES), jnp.float32)
    carry = lax.fori_loop(0, nt - 1, idx_body, (inf_acc, -inf_acc, zero_acc, zero_acc))
    t_last = nt - 1
    isc = idx_scores(t_last)
    valid = (row + t_last * KEY_TILE) <= qpos
    isc_m = jnp.where(valid, isc, -jnp.inf)
    sc_ref[pl.ds(t_last * KEY_TILE, KEY_TILE), :] = isc_m
    mn, mx, cge, cgt = idx_stats(carry, jnp.where(valid, isc, jnp.inf), isc_m)
    smin = jnp.min(mn, axis=0, keepdims=True)
    smax = jnp.max(mx, axis=0, keepdims=True)
    t_first = jnp.min(mx, axis=0, keepdims=True)
    cge0 = jnp.sum(cge, axis=0, keepdims=True)
    cgt0 = jnp.sum(cgt, axis=0, keepdims=True)

    def count_ge(thr):
        def body(t, acc):
            blk = sc_ref[pl.ds(t * KEY_TILE, KEY_TILE), :]
            return acc + fold((blk >= thr).astype(jnp.float32), jnp.add)
        acc = lax.fori_loop(0, nt, body, jnp.zeros((n_acc, LANES), jnp.float32))
        return jnp.sum(acc, axis=0, keepdims=True)

    def top_below(bound):
        n_set = 4

        def body(t, accs):
            blk = sc_ref[pl.ds(t * KEY_TILE, KEY_TILE), :].reshape(KEY_TILE // 8, 8, LANES)
            accs = [list(a) for a in accs]
            for v in range(KEY_TILE // 8):
                x = jnp.where(blk[v] < bound, blk[v], -jnp.inf)
                lv = accs[v % n_set]
                for i in range(TOP_R):
                    hi_ = jnp.maximum(lv[i], x)
                    if i < TOP_R - 1:
                        x = jnp.minimum(lv[i], x)
                    lv[i] = hi_
            return tuple(tuple(a) for a in accs)

        ninf = jnp.full((8, LANES), -jnp.inf, jnp.float32)
        accs = lax.fori_loop(0, nt, body, tuple(tuple(ninf for _ in range(TOP_R)) for _ in range(n_set)))
        cand = jnp.concatenate([a for s_ in accs for a in s_], axis=0)
        ridx = lax.broadcasted_iota(jnp.int32, cand.shape, 0).astype(jnp.float32)
        tops = []
        for _ in range(TOP_R):
            e = jnp.max(cand, axis=0, keepdims=True)
            first = jnp.min(jnp.where(cand == e, ridx, float(cand.shape[0])), axis=0, keepdims=True)
            cand = jnp.where(ridx == first, -jnp.inf, cand)
            tops.append(e)
        return tops

    def all_set(flag):
        return jnp.min(flag) > 0.5

    inf_row = jnp.full((1, LANES), jnp.inf, jnp.float32)

    def select_threshold():
        one = jnp.ones((1, LANES), jnp.float32)
        zero = jnp.zeros((1, LANES), jnp.float32)
        k_aim = kf + AIM_OFFSET

        def logc(c):
            return jnp.log(jnp.maximum(c, 0.5) / k_aim)

        n_valid = (qpos + 1).astype(jnp.float32)
        pos_k = cgt0 >= kf
        neg_k = cge0 < kf
        st = dict(
            lo=jnp.where(pos_k, 0.0, smin), flo=jnp.where(pos_k, logc(cgt0), logc(n_valid)),
            bhi=jnp.where(neg_k, 0.0, smax), fhi=jnp.where(neg_k, logc(cge0), logc(one)),
            hx=jnp.where(neg_k, 0.0, inf_row), chx=jnp.where(neg_k, cge0, zero),
            side=zero,
            found=jnp.where(pos_k | neg_k, zero, one),
            thr=zero, keep=jnp.where(pos_k | neg_k, inf_row, kf - cgt0))

        def probe(st, mid):
            c = count_ge(mid)
            hit = (c == kf) & (st["found"] < 0.5)
            ge = c >= kf
            f = logc(c)
            newside = jnp.where(ge, one, -one)
            same = newside == st["side"]
            fhi = jnp.where(ge & same, st["fhi"] * 0.5, st["fhi"])
            flo = jnp.where(jnp.logical_not(ge) & same, st["flo"] * 0.5, st["flo"])
            return dict(
                lo=jnp.where(ge, mid, st["lo"]), flo=jnp.where(ge, f, flo),
                bhi=jnp.where(ge, st["bhi"], mid), fhi=jnp.where(ge, fhi, f),
                hx=jnp.where(ge, st["hx"], mid), chx=jnp.where(ge, st["chx"], c),
                side=newside,
                found=jnp.where(hit, one, st["found"]),
                thr=jnp.where(hit, mid, st["thr"]), keep=st["keep"])

        def next_mid(st):
            lo, bhi = st["lo"], st["bhi"]
            den = st["flo"] - st["fhi"]
            mid = lo + (bhi - lo) * (st["flo"] / jnp.where(den == 0.0, 1.0, den))
            inside = (mid > lo) & (mid < bhi)
            return jnp.where(inside, mid, lo + (bhi - lo) * 0.5)

        def resolve_top(st):
            tops = top_below(st["hx"])
            r = kf - st["chx"]
            tr = tops[TOP_R - 1]
            for i in range(TOP_R - 2, -1, -1):
                tr = jnp.where(r == float(i + 1), tops[i], tr)
            n_gt = zero
            for i in range(TOP_R - 1):
                n_gt = n_gt + (tops[i] > tr).astype(jnp.float32)
            ok = (st["found"] < 0.5) & (r <= float(TOP_R))
            st = dict(st, found=jnp.where(ok, one, st["found"]), thr=jnp.where(ok, tr, st["thr"]),
                      keep=jnp.where(ok, r - n_gt, st["keep"]))
            return st, tops[0]

        first = jnp.minimum(jnp.maximum(t_first, smin), smax)
        inside = (first > st["lo"]) & (first < st["bhi"])
        st = probe(st, jnp.where(inside, first, next_mid(st)))
        st = dict(st, side=zero)
        st = lax.fori_loop(0, N_INTERP, lambda _, s_: probe(s_, next_mid(s_)), st)
        st, _ = resolve_top(st)

        def fb_body(st):
            st = probe(st, next_mid(st))
            st, u = resolve_top(st)
            cu = count_ge(u)
            ok = (st["found"] < 0.5) & (cu >= kf)
            go = (st["found"] < 0.5) & (cu < kf)
            return dict(
                lo=st["lo"], flo=st["flo"],
                bhi=jnp.where(go, jnp.minimum(st["bhi"], u), st["bhi"]),
                fhi=jnp.where(go, logc(cu), st["fhi"]),
                hx=jnp.where(go, u, st["hx"]), chx=jnp.where(go, cu, st["chx"]),
                side=st["side"],
                found=jnp.where(ok, one, st["found"]),
                thr=jnp.where(ok, u, st["thr"]), keep=jnp.where(ok, kf - st["chx"], st["keep"]))

        st = lax.while_loop(lambda s_: jnp.logical_not(all_set(s_["found"])), fb_body, st)
        return st["thr"], st["keep"]

    thr, keep = lax.cond((j + 1) * BLOCK > top_k, select_threshold,
                         lambda: (jnp.full((1, LANES), -F32_MAX, jnp.float32), inf_row))

    NH = N_DSA_HEADS

    def att_body(t, carry):
        m, l, acc, n_tie = carry
        kd = kd_ref[0, pl.ds(t * KEY_TILE, KEY_TILE), :]
        sd = lax.dot_general(kd, q_att, _NT, preferred_element_type=jnp.float32)
        blk = sc_ref[pl.ds(t * KEY_TILE, KEY_TILE), :]
        tie = blk == thr
        tie_rank = n_tie + jnp.dot(ltri_ref[...], tie.astype(jnp.float32).astype(jnp.bfloat16),
                                   preferred_element_type=jnp.float32)
        n_tie = tie_rank[KEY_TILE - 1:KEY_TILE, :]
        sel = (blk > thr) | (tie & (tie_rank <= keep))
        sd = jnp.concatenate(
            [jnp.where(sel, sd[:, h * LANES:(h + 1) * LANES], -jnp.inf) for h in range(NH)], axis=1)
        m_new = jnp.maximum(m, jnp.max(sd, axis=0, keepdims=True))
        alpha = jnp.exp(m - m_new)
        p = jnp.exp(sd - m_new)
        l = l * alpha + jnp.sum(p, axis=0, keepdims=True)
        pv = jnp.dot(vt_ref[0, t], p.astype(jnp.bfloat16), preferred_element_type=jnp.float32)
        return m_new, l, acc * alpha + pv, n_tie

    m0 = jnp.full((1, NH * LANES), NEG_BIG, jnp.float32)
    l0 = jnp.zeros((1, NH * LANES), jnp.float32)
    a0 = jnp.zeros((HEAD_DIM, NH * LANES), jnp.float32)
    _, l, acc, _ = lax.fori_loop(0, nt, att_body, (m0, l0, a0, jnp.zeros((1, LANES), jnp.float32)))
    ot = acc / l
    for p in range(NH // 2):
        pair = jnp.concatenate([ot[:, (2 * p) * LANES:(2 * p + 1) * LANES],
                                ot[:, (2 * p + 1) * LANES:(2 * p + 2) * LANES]], axis=0)
        o_ref[0, :, p * LANES:(p + 1) * LANES] = pair.T.astype(jnp.bfloat16)


def _dsa(qi, wit, qd, ki, kd, vdt, top_k):
    B, S, _ = qi.shape
    assert S % KEY_TILE == 0 and KEY_TILE % BLOCK == 0
    blk = lambda n: pl.BlockSpec((1, BLOCK, n), lambda b, j: (b, j, 0))
    whole = lambda n: pl.BlockSpec((1, S, n), lambda b, j: (b, 0, 0))
    ltri = jnp.tril(jnp.ones((KEY_TILE, KEY_TILE), jnp.bfloat16))
    return pl.pallas_call(
        functools.partial(_dsa_kernel, top_k=top_k),
        grid=(B, S // BLOCK),
        in_specs=[blk(2 * LANES),
                  pl.BlockSpec((1, 8, BLOCK), lambda b, j: (b, 0, j)),
                  blk(2 * LANES), whole(LANES), whole(LANES),
                  pl.BlockSpec((1, S // KEY_TILE, HEAD_DIM, KEY_TILE), lambda b, j: (b, 0, 0, 0)),
                  _const_spec(ltri.shape)],
        out_specs=blk(DSA_WIDTH),
        out_shape=jax.ShapeDtypeStruct((B, S, DSA_WIDTH), jnp.bfloat16),
        scratch_shapes=[pltpu.VMEM((S, LANES), jnp.float32)],
        compiler_params=_cparams(2),
        name="dsa",
    )(qi, wit, qd, ki, kd, vdt, ltri)


def _merge_kernel(x_ref, gates_ref, os_ref, od_ref, om_ref, wps_ref, wpd_ref, wpm_ref, wo_ref,
                  gm_ref, w1_ref, w2_ref, gf_ref, out_ref):
    f32 = jnp.float32
    x = x_ref[...]
    merged = None
    for b, (o_ref, w_ref) in enumerate(((os_ref, wps_ref), (od_ref, wpd_ref), (om_ref, wpm_ref))):
        y = jnp.dot(o_ref[...], w_ref[...], preferred_element_type=f32)
        y = y * gates_ref[:, b * D_MODEL:(b + 1) * D_MODEL].astype(f32)
        merged = y if merged is None else merged + y
    x1 = x + jnp.dot(merged.astype(jnp.bfloat16), wo_ref[...], preferred_element_type=f32)
    h = (x1 * _rms_scale(x1) * gm_ref[...]).astype(jnp.bfloat16)
    acc = x1
    FC = 1024
    for c in range(D_FF // FC):
        hid = jnp.maximum(jnp.dot(h, w1_ref[:, c * FC:(c + 1) * FC], preferred_element_type=f32), 0.0)
        hid = (hid * hid).astype(jnp.bfloat16)
        acc = acc + jnp.dot(hid, w2_ref[c * FC:(c + 1) * FC, :], preferred_element_type=f32)
    out_ref[...] = acc * _rms_scale(acc) * gf_ref[...]


def _merge(x2d, gates, o_swa, o_dsa, o_mem, w_proj_swa, w_proj_dsa, w_proj_mem, w_out,
           g_mlp, w_mlp_in, w_mlp_out, g_final):
    N = x2d.shape[0]
    T = TOK_TILE
    bf = jnp.bfloat16
    tok = lambda n: pl.BlockSpec((T, n), lambda i: (i, 0))
    ws = [w_proj_swa.astype(bf), w_proj_dsa.astype(bf), w_proj_mem.astype(bf), w_out.astype(bf)]
    w1, w2 = w_mlp_in.astype(bf), w_mlp_out.astype(bf)
    return pl.pallas_call(
        _merge_kernel,
        grid=(N // T,),
        in_specs=[tok(D_MODEL), tok(N_BRANCH * D_MODEL), tok(SWA_WIDTH), tok(DSA_WIDTH), tok(MEM_WIDTH)]
                 + [_const_spec(a.shape) for a in ws]
                 + [_const_spec((1, D_MODEL)), _const_spec(w1.shape), _const_spec(w2.shape),
                    _const_spec((1, D_MODEL))],
        out_specs=tok(D_MODEL),
        out_shape=jax.ShapeDtypeStruct((N, D_MODEL), jnp.float32),
        compiler_params=_cparams(1),
        name="merge",
    )(x2d, gates, o_swa, o_dsa, o_mem, *ws, g_mlp.reshape(1, D_MODEL), w1, w2,
      g_final.reshape(1, D_MODEL))


def kernel(x, mem, positions, g_mix, w_in, b_gate, sinks, g_mem, w_mem_kv, w_proj_swa, w_proj_dsa,
           w_proj_mem, w_out, g_mlp, w_mlp_in, w_mlp_out, g_final):
    B, S, D = x.shape
    assert g_mix.shape[0] == 1, "the final norm is fused into the single layer's merge kernel"
    top_k = min(TOPK_MAX, S // 4)
    for l in range(1):
        mk, mv = _memkv(mem, g_mem[l], w_mem_kv[l])
        (qs, ks, vs, qd, kd, qi, ki, vdt, wit, gates, o_mem) = _proj(
            x, positions, g_mix[l], w_in[l], b_gate[l], mk, mv)
        o_swa = _swa(qs, ks, vs, sinks[l])
        o_dsa = _dsa(qi, wit, qd, ki, kd, vdt, top_k)
        x = _merge(x.reshape(B * S, D), gates.reshape(B * S, -1), o_swa.reshape(B * S, -1),
                   o_dsa.reshape(B * S, -1), o_mem.reshape(B * S, -1), w_proj_swa[l], w_proj_dsa[l],
                   w_proj_mem[l], w_out[l], g_mlp[l], w_mlp_in[l], w_mlp_out[l], g_final).reshape(B, S, D)
    return x
```

```python
import functools

import jax
import jax.numpy as jnp
import numpy as np
from jax import lax
from jax.experimental import pallas as pl
from jax.experimental.pallas import tpu as pltpu

D_MODEL = 1024
HEAD_DIM = 64
N_SWA_HEADS = 8
N_SWA_KV = 2
WINDOW = 128
BLOCK = 128
N_DSA_HEADS = 4
N_IDX_HEADS = 4
IDX_DIM = 64
TOPK_MAX = 256
N_MEM_HEADS = 4
N_BRANCH = 3
D_FF = 4 * D_MODEL
ROPE_THETA = 10000.0
EPS = 1e-6

SWA_WIDTH = N_SWA_HEADS * HEAD_DIM
SWA_KV_WIDTH = N_SWA_KV * HEAD_DIM
DSA_WIDTH = N_DSA_HEADS * HEAD_DIM
MEM_WIDTH = N_MEM_HEADS * HEAD_DIM
IN_SPLITS = (SWA_WIDTH, SWA_KV_WIDTH, SWA_KV_WIDTH,
             DSA_WIDTH, HEAD_DIM, HEAD_DIM,
             N_IDX_HEADS * IDX_DIM, IDX_DIM, N_IDX_HEADS,
             MEM_WIDTH, N_BRANCH * D_MODEL)

LANES = 128
TOK_TILE = 512
KEY_TILE = 512
SWA_TILE = 512
VMEM_LIMIT = 56 * 1024 * 1024
NEG_BIG = -1e30
F32_MAX = float(np.finfo(np.float32).max)
N_INTERP = 6
TOP_R = 4
AIM_OFFSET = -2.5

ATT_SCALE = HEAD_DIM ** -0.5
IDX_SCALE = (N_IDX_HEADS ** -0.5) * (IDX_DIM ** -0.5)

G_QS, G_KS, G_QD, G_KD, G_QI, G_KI = 0, 4, 6, 8, 9, 11
N_ROPE_GROUPS = 12
G_VS, G_QM = 12, 14
N_RM_GROUPS = 16

WT_ROWS = 80
VT_ROWS = 80
LOG2E = float(np.log2(np.e))

_NT = (((1,), (1,)), ((), ()))


def _cparams(n_grid):
    return pltpu.CompilerParams(
        dimension_semantics=("arbitrary",) * n_grid, vmem_limit_bytes=VMEM_LIMIT)


def _const_spec(shape):
    nd = len(shape)
    return pl.BlockSpec(shape, lambda *_: (0,) * nd, pipeline_mode=pl.Buffered(1))


def _rms_scale(x):
    return lax.rsqrt(jnp.mean(x * x, axis=-1, keepdims=True) + EPS)


def _first_half_mask():
    lane = lax.broadcasted_iota(jnp.int32, (1, LANES), 1)
    return (lane % HEAD_DIM) < (HEAD_DIM // 2), lane < HEAD_DIM


def _stack_heads(q, n_heads, low64):
    parts = []
    for h in range(n_heads):
        col = q[:, (h // 2) * LANES:(h // 2 + 1) * LANES]
        keep = low64 if h % 2 == 0 else jnp.logical_not(low64)
        parts.append(jnp.where(keep, col, jnp.zeros_like(col)))
    return jnp.concatenate(parts, axis=0)


def _memkv_kernel(mem_ref, g_ref, w_ref, k_ref, v_ref):
    m = mem_ref[0]
    mn = (m * _rms_scale(m) * g_ref[...]).astype(jnp.bfloat16)
    kv = jnp.dot(mn, w_ref[...], preferred_element_type=jnp.float32)
    k_ref[0] = kv[:, :MEM_WIDTH].astype(jnp.bfloat16)
    v_ref[0] = kv[:, MEM_WIDTH:].astype(jnp.bfloat16)


def _memkv(mem, g_mem, w_mem_kv):
    B, M, _ = mem.shape
    return pl.pallas_call(
        _memkv_kernel,
        grid=(B,),
        in_specs=[pl.BlockSpec((1, M, D_MODEL), lambda b: (b, 0, 0)),
                  _const_spec((1, D_MODEL)),
                  _const_spec((D_MODEL, 2 * MEM_WIDTH))],
        out_specs=[pl.BlockSpec((1, M, MEM_WIDTH), lambda b: (b, 0, 0)),
                   pl.BlockSpec((1, M, MEM_WIDTH), lambda b: (b, 0, 0))],
        out_shape=[jax.ShapeDtypeStruct((B, M, MEM_WIDTH), jnp.bfloat16)] * 2,
        compiler_params=_cparams(1),
        name="memkv",
    )(mem, g_mem.reshape(1, D_MODEL), w_mem_kv.astype(jnp.bfloat16))


def _proj_kernel(x_ref, pos_ref, g_ref, invf_ref, wrm_ref, wt_ref, wg_ref, bg_ref, mk_ref, mv_ref,
                 qs_ref, ks_ref, vs_ref, qd_ref, kd_ref, qi_ref, ki_ref, vdt_ref, wit_ref,
                 gates_ref, omem_ref):
    x = x_ref[0]
    h = (x * _rms_scale(x) * g_ref[...]).astype(jnp.bfloat16)
    half32, low64 = _first_half_mask()

    ang = pos_ref[0].astype(jnp.float32) * invf_ref[...]
    cos = jnp.cos(ang)
    sin = jnp.where(half32, -jnp.sin(ang), jnp.sin(ang))

    def rope(y):
        partner = jnp.where(half32, pltpu.roll(y, LANES - HEAD_DIM // 2, 1),
                            pltpu.roll(y, HEAD_DIM // 2, 1))
        return y * cos + partner * sin

    dest = {}
    for g0, (ref, n) in {G_QS: (qs_ref, 4), G_KS: (ks_ref, 2), G_QD: (qd_ref, 2), G_KD: (kd_ref, 1),
                         G_QI: (qi_ref, 2), G_KI: (ki_ref, 1), G_VS: (vs_ref, 2)}.items():
        for i in range(n):
            dest[g0 + i] = (ref, i)
    qm_pairs = {}
    for g in range(0, N_RM_GROUPS, 2):
        y2 = jnp.dot(h, wrm_ref[:, g * LANES:(g + 2) * LANES], preferred_element_type=jnp.float32)
        for u in range(2):
            y = y2[:, u * LANES:(u + 1) * LANES]
            if g + u >= G_QM:
                qm_pairs[g + u - G_QM] = y.astype(jnp.bfloat16)
                continue
            if g + u < N_ROPE_GROUPS:
                y = rope(y)
            ref, i = dest[g + u]
            ref[0, :, i * LANES:(i + 1) * LANES] = y.astype(jnp.bfloat16)

    yt = lax.dot_general(wt_ref[...], h, _NT, preferred_element_type=jnp.float32)
    vdt_ref[0, 0, :HEAD_DIM, :] = yt[:HEAD_DIM].astype(jnp.bfloat16)
    pad_row = lax.broadcasted_iota(jnp.int32, (VT_ROWS - HEAD_DIM, yt.shape[1]), 0)
    vdt_ref[0, 0, HEAD_DIM:, :] = jnp.where(pad_row == 0, 1.0, 0.0).astype(jnp.bfloat16)
    wit_ref[0] = yt[HEAD_DIM:HEAD_DIM + 8]

    n_gc = (N_BRANCH * D_MODEL) // 512
    for c in range(n_gc):
        z = jnp.dot(h, wg_ref[:, c * 512:(c + 1) * 512], preferred_element_type=jnp.float32)
        z = z + bg_ref[:, c * 512:(c + 1) * 512]
        gates_ref[0, :, c * 512:(c + 1) * 512] = jax.nn.sigmoid(z).astype(jnp.bfloat16)

    for p in range(N_MEM_HEADS // 2):
        qm = qm_pairs[p]
        mk = mk_ref[0, :, p * LANES:(p + 1) * LANES]
        mv = mv_ref[0, :, p * LANES:(p + 1) * LANES]
        o_pair = None
        for hh in range(2):
            keep = low64 if hh == 0 else jnp.logical_not(low64)
            qh = jnp.where(keep, qm, jnp.zeros_like(qm))
            s = lax.dot_general(qh, mk, _NT, preferred_element_type=jnp.float32)
            e = jnp.exp(s - jnp.max(s, axis=-1, keepdims=True))
            l = jnp.sum(e, axis=-1, keepdims=True)
            o = jnp.dot(e.astype(jnp.bfloat16), mv, preferred_element_type=jnp.float32) / l
            o_pair = o if hh == 0 else jnp.where(low64, o_pair, o)
        omem_ref[0, :, p * LANES:(p + 1) * LANES] = o_pair.astype(jnp.bfloat16)


def _proj(x, positions, g_mix, w_in, b_gate, mk, mv):
    B, S, _ = x.shape
    M = mk.shape[1]
    T = TOK_TILE
    assert S % T == 0
    sp = np.cumsum((0,) + IN_SPLITS)
    w = {n: w_in[:, sp[i]:sp[i + 1]] for i, n in enumerate(
        ("qs", "ks", "vs", "qd", "kd", "vd", "qi", "ki", "wi", "qm", "gate"))}

    def dup(a):
        return jnp.concatenate([a, a], axis=1)

    wrm = jnp.concatenate(
        [w["qs"] * ATT_SCALE,
         dup(w["ks"][:, :HEAD_DIM]), dup(w["ks"][:, HEAD_DIM:]),
         w["qd"] * (ATT_SCALE * LOG2E), dup(w["kd"]),
         w["qi"], dup(w["ki"]),
         dup(w["vs"][:, :HEAD_DIM]), dup(w["vs"][:, HEAD_DIM:]),
         w["qm"] * ATT_SCALE], axis=1).astype(jnp.bfloat16)
    assert wrm.shape[1] == N_RM_GROUPS * LANES
    wt = jnp.concatenate(
        [w["vd"].T, w["wi"].T * IDX_SCALE, jnp.zeros((WT_ROWS - HEAD_DIM - N_IDX_HEADS, D_MODEL), w_in.dtype)],
        axis=0).astype(jnp.bfloat16)
    wg = w["gate"].astype(jnp.bfloat16)
    half = HEAD_DIM // 2
    invf = jnp.power(ROPE_THETA, -jnp.arange(half, dtype=jnp.float32) / half)
    invf = jnp.tile(invf, LANES // half).reshape(1, LANES)

    tok = lambda n: pl.BlockSpec((1, T, n), lambda b, i: (b, i, 0))
    bf = jnp.bfloat16
    out_shape = [
        jax.ShapeDtypeStruct((B, S, 4 * LANES), bf),
        jax.ShapeDtypeStruct((B, S, 2 * LANES), bf),
        jax.ShapeDtypeStruct((B, S, 2 * LANES), bf),
        jax.ShapeDtypeStruct((B, S, 2 * LANES), bf),
        jax.ShapeDtypeStruct((B, S, LANES), bf),
        jax.ShapeDtypeStruct((B, S, 2 * LANES), bf),
        jax.ShapeDtypeStruct((B, S, LANES), bf),
        jax.ShapeDtypeStruct((B, S // T, VT_ROWS, T), bf),
        jax.ShapeDtypeStruct((B, 8, S), jnp.float32),
        jax.ShapeDtypeStruct((B, S, N_BRANCH * D_MODEL), bf),
        jax.ShapeDtypeStruct((B, S, MEM_WIDTH), bf),
    ]
    out_specs = [tok(4 * LANES), tok(2 * LANES), tok(2 * LANES), tok(2 * LANES), tok(LANES),
                 tok(2 * LANES), tok(LANES),
                 pl.BlockSpec((1, 1, VT_ROWS, T), lambda b, i: (b, i, 0, 0)),
                 pl.BlockSpec((1, 8, T), lambda b, i: (b, 0, i)),
                 tok(N_BRANCH * D_MODEL), tok(MEM_WIDTH)]
    return pl.pallas_call(
        _proj_kernel,
        grid=(B, S // T),
        in_specs=[tok(D_MODEL),
                  pl.BlockSpec((1, T, 1), lambda b, i: (b, i, 0)),
                  _const_spec((1, D_MODEL)), _const_spec((1, LANES)),
                  _const_spec(wrm.shape), _const_spec(wt.shape), _const_spec(wg.shape),
                  _const_spec((1, N_BRANCH * D_MODEL)),
                  pl.BlockSpec((1, M, MEM_WIDTH), lambda b, i: (b, 0, 0)),
                  pl.BlockSpec((1, M, MEM_WIDTH), lambda b, i: (b, 0, 0))],
        out_specs=out_specs,
        out_shape=out_shape,
        compiler_params=_cparams(2),
        name="proj",
    )(x, positions.reshape(B, S, 1), g_mix.reshape(1, D_MODEL), invf, wrm, wt, wg,
      b_gate.reshape(1, -1), mk, mv)


def _swa_kernel(sink_ref, q_ref, kc_ref, kp_ref, vc_ref, vp_ref, o_ref):
    i = pl.program_id(1)
    _, low64 = _first_half_mask()
    q = q_ref[0]
    kcat = jnp.concatenate([kp_ref[0], kc_ref[0]], axis=0)
    vcat = jnp.concatenate([vp_ref[0], vc_ref[0]], axis=0)
    qi = lax.broadcasted_iota(jnp.int32, (BLOCK, 2 * BLOCK), 0)
    sj = lax.broadcasted_iota(jnp.int32, (BLOCK, 2 * BLOCK), 1)
    dist = qi + BLOCK - sj
    band = (dist >= 0) & (dist < WINDOW)
    G = N_SWA_HEADS // N_SWA_KV
    for bq in range(SWA_TILE // BLOCK):
        valid = band
        if bq == 0:
            valid = band & ((sj + jnp.where(i > 0, BLOCK, 0)) >= BLOCK)
        qb = q[bq * BLOCK:(bq + 1) * BLOCK]
        for g in range(N_SWA_KV):
            qst = _stack_heads(qb[:, g * G * HEAD_DIM:(g + 1) * G * HEAD_DIM], G, low64)
            kw = kcat[bq * BLOCK:(bq + 2) * BLOCK, g * LANES:(g + 1) * LANES]
            vw = vcat[bq * BLOCK:(bq + 2) * BLOCK, g * LANES:(g + 1) * LANES]
            s = lax.dot_general(qst, kw, _NT, preferred_element_type=jnp.float32)
            es, ls = [], []
            for hh in range(G):
                sh = jnp.where(valid, s[hh * BLOCK:(hh + 1) * BLOCK], -jnp.inf)
                sink = sink_ref[g * G + hh]
                m = jnp.maximum(jnp.max(sh, axis=-1, keepdims=True), sink)
                e = jnp.exp(sh - m)
                ls.append(jnp.sum(e, axis=-1, keepdims=True) + jnp.exp(sink - m))
                es.append(e.astype(jnp.bfloat16))
            o = jnp.dot(jnp.concatenate(es, axis=0), vw, preferred_element_type=jnp.float32)
            for pp in range(G // 2):
                o0 = o[(2 * pp) * BLOCK:(2 * pp + 1) * BLOCK] / ls[2 * pp]
                o1 = o[(2 * pp + 1) * BLOCK:(2 * pp + 2) * BLOCK] / ls[2 * pp + 1]
                c0 = (g * (G // 2) + pp) * LANES
                o_ref[0, bq * BLOCK:(bq + 1) * BLOCK, c0:c0 + LANES] = (
                    jnp.where(low64, o0, o1).astype(jnp.bfloat16))


def _swa(qs, ks, vs, sinks):
    B, S, _ = qs.shape
    TQ = SWA_TILE
    r = TQ // BLOCK
    cur = lambda n: pl.BlockSpec((1, TQ, n), lambda b, i: (b, i, 0))
    prev = lambda n: pl.BlockSpec((1, BLOCK, n), lambda b, i: (b, jnp.maximum(i * r - 1, 0), 0))
    return pl.pallas_call(
        _swa_kernel,
        grid=(B, S // TQ),
        in_specs=[pl.BlockSpec(memory_space=pltpu.SMEM),
                  cur(4 * LANES), cur(2 * LANES), prev(2 * LANES), cur(2 * LANES), prev(2 * LANES)],
        out_specs=cur(SWA_WIDTH),
        out_shape=jax.ShapeDtypeStruct((B, S, SWA_WIDTH), jnp.bfloat16),
        compiler_params=_cparams(2),
        name="swa",
    )(sinks, qs, ks, ks, vs, vs)


def _dsa_kernel(qi_ref, wi_ref, qd_ref, ki_ref, kd_ref, vt_ref, ltri_ref, o_ref, sc_ref, sta_ref, stb_ref,
                *, top_k):
    j = pl.program_id(1)
    nt = j // (KEY_TILE // BLOCK) + 1
    _, low64 = _first_half_mask()
    lane = lax.broadcasted_iota(jnp.int32, (1, LANES), 1)
    qpos = j * BLOCK + lane
    w = wi_ref[0]
    q_idx = _stack_heads(qi_ref[0], N_IDX_HEADS, low64)
    q_att = _stack_heads(qd_ref[0], N_DSA_HEADS, low64)
    kf = jnp.float32(top_k)
    row = lax.broadcasted_iota(jnp.int32, (KEY_TILE, LANES), 0)
    n_acc = 64

    def fold(a, op):
        a = a.reshape(KEY_TILE // n_acc, n_acc, LANES)
        r = a[0]
        for u in range(1, KEY_TILE // n_acc):
            r = op(r, a[u])
        return r

    def idx_scores(t):
        ks = ki_ref[0, pl.ds(t * KEY_TILE, KEY_TILE), :]
        st = lax.dot_general(ks, q_idx, _NT, preferred_element_type=jnp.float32)
        isc = None
        for h in range(N_IDX_HEADS):
            term = jnp.maximum(st[:, h * LANES:(h + 1) * LANES], 0.0) * w[h:h + 1, :]
            isc = term if isc is None else isc + term
        return isc

    def idx_stats(carry, lo_src, isc):
        mn, mx, cge, cgt = carry
        return (jnp.minimum(mn, fold(lo_src, jnp.minimum)),
                jnp.maximum(mx, fold(isc, jnp.maximum)),
                cge + fold((isc >= 0.0).astype(jnp.float32), jnp.add),
                cgt + fold((isc > 0.0).astype(jnp.float32), jnp.add))

    def idx_body(t, carry):
        isc = idx_scores(t)
        sc_ref[pl.ds(t * KEY_TILE, KEY_TILE), :] = isc
        return idx_stats(carry, isc, isc)

    inf_acc = jnp.full((n_acc, LANES), jnp.inf, jnp.float32)
    zero_acc = jnp.zeros((n_acc, LANES), jnp.float32)
    carry = lax.fori_loop(0, nt - 1, idx_body, (inf_acc, -inf_acc, zero_acc, zero_acc))
    t_last = nt - 1
    isc = idx_scores(t_last)
    valid = (row + t_last * KEY_TILE) <= qpos
    isc_m = jnp.where(valid, isc, -jnp.inf)
    sc_ref[pl.ds(t_last * KEY_TILE, KEY_TILE), :] = isc_m
    mn, mx, cge, cgt = idx_stats(carry, jnp.where(valid, isc, jnp.inf), isc_m)
    smin = jnp.min(mn, axis=0, keepdims=True)
    smax = jnp.max(mx, axis=0, keepdims=True)
    t_first = jnp.min(mx, axis=0, keepdims=True)
    cge0 = jnp.sum(cge, axis=0, keepdims=True)
    cgt0 = jnp.sum(cgt, axis=0, keepdims=True)

    def count_ge(thr):
        def body(t, acc):
            blk = sc_ref[pl.ds(t * KEY_TILE, KEY_TILE), :]
            return acc + fold((blk >= thr).astype(jnp.float32), jnp.add)
        acc = lax.fori_loop(0, nt, body, jnp.zeros((n_acc, LANES), jnp.float32))
        return jnp.sum(acc, axis=0, keepdims=True)

    def top_below(bound):
        n_set = 4

        def body(t, accs):
            blk = sc_ref[pl.ds(t * KEY_TILE, KEY_TILE), :].reshape(KEY_TILE // 8, 8, LANES)
            accs = [list(a) for a in accs]
            for v in range(KEY_TILE // 8):
                x = jnp.where(blk[v] < bound, blk[v], -jnp.inf)
                lv = accs[v % n_set]
                for i in range(TOP_R):
                    hi_ = jnp.maximum(lv[i], x)
                    if i < TOP_R - 1:
                        x = jnp.minimum(lv[i], x)
                    lv[i] = hi_
            return tuple(tuple(a) for a in accs)

        ninf = jnp.full((8, LANES), -jnp.inf, jnp.float32)
        accs = lax.fori_loop(0, nt, body, tuple(tuple(ninf for _ in range(TOP_R)) for _ in range(n_set)))
        cand = jnp.concatenate([a for s_ in accs for a in s_], axis=0)
        ridx = lax.broadcasted_iota(jnp.int32, cand.shape, 0).astype(jnp.float32)
        tops = []
        for _ in range(TOP_R):
            e = jnp.max(cand, axis=0, keepdims=True)
            first = jnp.min(jnp.where(cand == e, ridx, float(cand.shape[0])), axis=0, keepdims=True)
            cand = jnp.where(ridx == first, -jnp.inf, cand)
            tops.append(e)
        return tops

    def all_set(flag):
        return jnp.min(flag) > 0.5

    inf_row = jnp.full((1, LANES), jnp.inf, jnp.float32)

    def select_threshold():
        one = jnp.ones((1, LANES), jnp.float32)
        zero = jnp.zeros((1, LANES), jnp.float32)
        k_aim = kf + AIM_OFFSET

        def logc(c):
            return jnp.log(jnp.maximum(c, 0.5) / k_aim)

        n_valid = (qpos + 1).astype(jnp.float32)
        pos_k = cgt0 >= kf
        neg_k = cge0 < kf
        st = dict(
            lo=jnp.where(pos_k, 0.0, smin), flo=jnp.where(pos_k, logc(cgt0), logc(n_valid)),
            bhi=jnp.where(neg_k, 0.0, smax), fhi=jnp.where(neg_k, logc(cge0), logc(one)),
            hx=jnp.where(neg_k, 0.0, inf_row), chx=jnp.where(neg_k, cge0, zero),
            side=zero,
            found=jnp.where(pos_k | neg_k, zero, one),
            thr=zero, keep=jnp.where(pos_k | neg_k, inf_row, kf - cgt0))

        def probe(st, mid):
            c = count_ge(mid)
            hit = (c == kf) & (st["found"] < 0.5)
            ge = c >= kf
            f = logc(c)
            newside = jnp.where(ge, one, -one)
            same = newside == st["side"]
            fhi = jnp.where(ge & same, st["fhi"] * 0.5, st["fhi"])
            flo = jnp.where(jnp.logical_not(ge) & same, st["flo"] * 0.5, st["flo"])
            return dict(
                lo=jnp.where(ge, mid, st["lo"]), flo=jnp.where(ge, f, flo),
                bhi=jnp.where(ge, st["bhi"], mid), fhi=jnp.where(ge, fhi, f),
                hx=jnp.where(ge, st["hx"], mid), chx=jnp.where(ge, st["chx"], c),
                side=newside,
                found=jnp.where(hit, one, st["found"]),
                thr=jnp.where(hit, mid, st["thr"]), keep=st["keep"])

        def next_mid(st):
            lo, bhi = st["lo"], st["bhi"]
            den = st["flo"] - st["fhi"]
            mid = lo + (bhi - lo) * (st["flo"] / jnp.where(den == 0.0, 1.0, den))
            inside = (mid > lo) & (mid < bhi)
            return jnp.where(inside, mid, lo + (bhi - lo) * 0.5)

        def resolve_top(st):
            tops = top_below(st["hx"])
            r = kf - st["chx"]
            tr = tops[TOP_R - 1]
            for i in range(TOP_R - 2, -1, -1):
                tr = jnp.where(r == float(i + 1), tops[i], tr)
            n_gt = zero
            for i in range(TOP_R - 1):
                n_gt = n_gt + (tops[i] > tr).astype(jnp.float32)
            ok = (st["found"] < 0.5) & (r <= float(TOP_R))
            st = dict(st, found=jnp.where(ok, one, st["found"]), thr=jnp.where(ok, tr, st["thr"]),
                      keep=jnp.where(ok, r - n_gt, st["keep"]))
            return st, tops[0]

        first = jnp.minimum(jnp.maximum(t_first, smin), smax)
        inside = (first > st["lo"]) & (first < st["bhi"])
        st = probe(st, jnp.where(inside, first, next_mid(st)))
        st = dict(st, side=zero)
        st = lax.fori_loop(0, N_INTERP, lambda _, s_: probe(s_, next_mid(s_)), st)
        st, _ = resolve_top(st)

        def fb_body(st):
            st = probe(st, next_mid(st))
            st, u = resolve_top(st)
            cu = count_ge(u)
            ok = (st["found"] < 0.5) & (cu >= kf)
            go = (st["found"] < 0.5) & (cu < kf)
            return dict(
                lo=st["lo"], flo=st["flo"],
                bhi=jnp.where(go, jnp.minimum(st["bhi"], u), st["bhi"]),
                fhi=jnp.where(go, logc(cu), st["fhi"]),
                hx=jnp.where(go, u, st["hx"]), chx=jnp.where(go, cu, st["chx"]),
                side=st["side"],
                found=jnp.where(ok, one, st["found"]),
                thr=jnp.where(ok, u, st["thr"]), keep=jnp.where(ok, kf - st["chx"], st["keep"]))

        st = lax.while_loop(lambda s_: jnp.logical_not(all_set(s_["found"])), fb_body, st)
        return st["thr"], st["keep"]

    thr, keep = lax.cond((j + 1) * BLOCK > top_k, select_threshold,
                         lambda: (jnp.full((1, LANES), -F32_MAX, jnp.float32), inf_row))

    NH = N_DSA_HEADS

    def score_stage(t, n_tie, st_ref):
        kd = kd_ref[0, pl.ds(t * KEY_TILE, KEY_TILE), :]
        sd = lax.dot_general(kd, q_att, _NT, preferred_element_type=jnp.float32)
        blk = sc_ref[pl.ds(t * KEY_TILE, KEY_TILE), :]
        tie = blk == thr
        tie_bf = tie.astype(jnp.float32).astype(jnp.bfloat16)
        ranks = []
        for sb in range(KEY_TILE // BLOCK):
            rk = n_tie + jnp.dot(ltri_ref[...], tie_bf[sb * BLOCK:(sb + 1) * BLOCK],
                                 preferred_element_type=jnp.float32)
            n_tie = rk[BLOCK - 1:BLOCK, :]
            ranks.append(rk)
        sel = (blk > thr) | (tie & (jnp.concatenate(ranks, axis=0) <= keep))
        cms = []
        for h in range(NH):
            sdm = jnp.where(sel, sd[:, h * LANES:(h + 1) * LANES], -jnp.inf)
            st_ref[:, h * LANES:(h + 1) * LANES] = sdm
            cms.append(jnp.max(sdm, axis=0, keepdims=True))
        return jnp.concatenate(cms, axis=1), n_tie

    def value_stage(t, m, acc, cmax, st_ref):
        m_new = jnp.maximum(m, cmax)
        p = jnp.exp2(st_ref[...] - m_new).astype(jnp.bfloat16)
        pv = jnp.dot(vt_ref[0, t], p, preferred_element_type=jnp.float32)
        return m_new, acc * jnp.exp2(m - m_new) + pv

    def pair_body(u, carry):
        m, acc, cmax, n_tie = carry
        t = 2 * u
        cmax_b, n_tie = score_stage(t + 1, n_tie, stb_ref)
        m, acc = value_stage(t, m, acc, cmax, sta_ref)
        cmax, n_tie = score_stage(t + 2, n_tie, sta_ref)
        m, acc = value_stage(t + 1, m, acc, cmax_b, stb_ref)
        return m, acc, cmax, n_tie

    m0 = jnp.full((1, NH * LANES), NEG_BIG, jnp.float32)
    a0 = jnp.zeros((VT_ROWS, NH * LANES), jnp.float32)
    cmax, n_tie = score_stage(0, jnp.zeros((1, LANES), jnp.float32), sta_ref)
    n_pairs = (nt - 1) // 2
    m, acc, cmax, n_tie = lax.fori_loop(0, n_pairs, pair_body, (m0, a0, cmax, n_tie))
    t_rem = 2 * n_pairs

    def last_two():
        cmax_l, _ = score_stage(t_rem + 1, n_tie, stb_ref)
        m_, acc_ = value_stage(t_rem, m, acc, cmax, sta_ref)
        return value_stage(t_rem + 1, m_, acc_, cmax_l, stb_ref)[1]

    acc = lax.cond(nt - t_rem == 2, last_two, lambda: value_stage(t_rem, m, acc, cmax, sta_ref)[1])
    ot = acc[:HEAD_DIM] / acc[HEAD_DIM:HEAD_DIM + 1]
    for p in range(NH // 2):
        pair = jnp.concatenate([ot[:, (2 * p) * LANES:(2 * p + 1) * LANES],
                                ot[:, (2 * p + 1) * LANES:(2 * p + 2) * LANES]], axis=0)
        o_ref[0, :, p * LANES:(p + 1) * LANES] = pair.T.astype(jnp.bfloat16)


def _dsa(qi, wit, qd, ki, kd, vdt, top_k):
    B, S, _ = qi.shape
    assert S % KEY_TILE == 0 and KEY_TILE % BLOCK == 0
    blk = lambda n: pl.BlockSpec((1, BLOCK, n), lambda b, j: (b, j, 0))
    whole = lambda n: pl.BlockSpec((1, S, n), lambda b, j: (b, 0, 0))
    ltri = jnp.tril(jnp.ones((BLOCK, BLOCK), jnp.bfloat16))
    return pl.pallas_call(
        functools.partial(_dsa_kernel, top_k=top_k),
        grid=(B, S // BLOCK),
        in_specs=[blk(2 * LANES),
                  pl.BlockSpec((1, 8, BLOCK), lambda b, j: (b, 0, j)),
                  blk(2 * LANES), whole(LANES), whole(LANES),
                  pl.BlockSpec((1, S // KEY_TILE, VT_ROWS, KEY_TILE), lambda b, j: (b, 0, 0, 0)),
                  _const_spec(ltri.shape)],
        out_specs=blk(DSA_WIDTH),
        out_shape=jax.ShapeDtypeStruct((B, S, DSA_WIDTH), jnp.bfloat16),
        scratch_shapes=[pltpu.VMEM((S, LANES), jnp.float32),
                        pltpu.VMEM((KEY_TILE, N_DSA_HEADS * LANES), jnp.float32),
                        pltpu.VMEM((KEY_TILE, N_DSA_HEADS * LANES), jnp.float32)],
        compiler_params=_cparams(2),
        name="dsa",
    )(qi, wit, qd, ki, kd, vdt, ltri)


def _merge_kernel(x_ref, gates_ref, os_ref, od_ref, om_ref, wps_ref, wpd_ref, wpm_ref, wo_ref,
                  gm_ref, w1_ref, w2_ref, gf_ref, out_ref):
    f32 = jnp.float32
    x = x_ref[...]
    merged = None
    for b, (o_ref, w_ref) in enumerate(((os_ref, wps_ref), (od_ref, wpd_ref), (om_ref, wpm_ref))):
        y = jnp.dot(o_ref[...], w_ref[...], preferred_element_type=f32)
        y = y * gates_ref[:, b * D_MODEL:(b + 1) * D_MODEL].astype(f32)
        merged = y if merged is None else merged + y
    x1 = x + jnp.dot(merged.astype(jnp.bfloat16), wo_ref[...], preferred_element_type=f32)
    h = (x1 * _rms_scale(x1) * gm_ref[...]).astype(jnp.bfloat16)
    acc = x1
    FC = 1024
    for c in range(D_FF // FC):
        hid = jnp.maximum(jnp.dot(h, w1_ref[:, c * FC:(c + 1) * FC], preferred_element_type=f32), 0.0)
        hid = (hid * hid).astype(jnp.bfloat16)
        acc = acc + jnp.dot(hid, w2_ref[c * FC:(c + 1) * FC, :], preferred_element_type=f32)
    out_ref[...] = acc * _rms_scale(acc) * gf_ref[...]


def _merge(x2d, gates, o_swa, o_dsa, o_mem, w_proj_swa, w_proj_dsa, w_proj_mem, w_out,
           g_mlp, w_mlp_in, w_mlp_out, g_final):
    N = x2d.shape[0]
    T = TOK_TILE
    bf = jnp.bfloat16
    tok = lambda n: pl.BlockSpec((T, n), lambda i: (i, 0))
    ws = [w_proj_swa.astype(bf), w_proj_dsa.astype(bf), w_proj_mem.astype(bf), w_out.astype(bf)]
    w1, w2 = w_mlp_in.astype(bf), w_mlp_out.astype(bf)
    return pl.pallas_call(
        _merge_kernel,
        grid=(N // T,),
        in_specs=[tok(D_MODEL), tok(N_BRANCH * D_MODEL), tok(SWA_WIDTH), tok(DSA_WIDTH), tok(MEM_WIDTH)]
                 + [_const_spec(a.shape) for a in ws]
                 + [_const_spec((1, D_MODEL)), _const_spec(w1.shape), _const_spec(w2.shape),
                    _const_spec((1, D_MODEL))],
        out_specs=tok(D_MODEL),
        out_shape=jax.ShapeDtypeStruct((N, D_MODEL), jnp.float32),
        compiler_params=_cparams(1),
        name="merge",
    )(x2d, gates, o_swa, o_dsa, o_mem, *ws, g_mlp.reshape(1, D_MODEL), w1, w2,
      g_final.reshape(1, D_MODEL))


def kernel(x, mem, positions, g_mix, w_in, b_gate, sinks, g_mem, w_mem_kv, w_proj_swa, w_proj_dsa,
           w_proj_mem, w_out, g_mlp, w_mlp_in, w_mlp_out, g_final):
    B, S, D = x.shape
    assert g_mix.shape[0] == 1, "the final norm is fused into the single layer's merge kernel"
    top_k = min(TOPK_MAX, S // 4)
    for l in range(1):
        mk, mv = _memkv(mem, g_mem[l], w_mem_kv[l])
        (qs, ks, vs, qd, kd, qi, ki, vdt, wit, gates, o_mem) = _proj(
            x, positions, g_mix[l], w_in[l], b_gate[l], mk, mv)
        o_swa = _swa(qs, ks, vs, sinks[l])
        o_dsa = _dsa(qi, wit, qd, ki, kd, vdt, top_k)
        x = _merge(x.reshape(B * S, D), gates.reshape(B * S, -1), o_swa.reshape(B * S, -1),
                   o_dsa.reshape(B * S, -1), o_mem.reshape(B * S, -1), w_proj_swa[l], w_proj_dsa[l],
                   w_proj_mem[l], w_out[l], g_mlp[l], w_mlp_in[l], w_mlp_out[l], g_final).reshape(B, S, D)
    return x
```

```python
import functools

import jax
import jax.numpy as jnp
import numpy as np
from jax import lax
from jax.experimental import pallas as pl
from jax.experimental.pallas import tpu as pltpu

D_MODEL = 1024
HEAD_DIM = 64
N_SWA_HEADS = 8
N_SWA_KV = 2
WINDOW = 128
BLOCK = 128
N_DSA_HEADS = 4
N_IDX_HEADS = 4
IDX_DIM = 64
TOPK_MAX = 256
N_MEM_HEADS = 4
N_BRANCH = 3
D_FF = 4 * D_MODEL
ROPE_THETA = 10000.0
EPS = 1e-6

SWA_WIDTH = N_SWA_HEADS * HEAD_DIM
SWA_KV_WIDTH = N_SWA_KV * HEAD_DIM
DSA_WIDTH = N_DSA_HEADS * HEAD_DIM
MEM_WIDTH = N_MEM_HEADS * HEAD_DIM
IN_SPLITS = (SWA_WIDTH, SWA_KV_WIDTH, SWA_KV_WIDTH,
             DSA_WIDTH, HEAD_DIM, HEAD_DIM,
             N_IDX_HEADS * IDX_DIM, IDX_DIM, N_IDX_HEADS,
             MEM_WIDTH, N_BRANCH * D_MODEL)

LANES = 128
TOK_TILE = 512
KEY_TILE = 512
SWA_TILE = 512
DSA_Q = 128
VMEM_LIMIT = 56 * 1024 * 1024
NEG_BIG = -1e30
F32_MAX = float(np.finfo(np.float32).max)
N_INTERP = 6
TOP_R = 4
AIM_OFFSET = -2.5

ATT_SCALE = HEAD_DIM ** -0.5
IDX_SCALE = (N_IDX_HEADS ** -0.5) * (IDX_DIM ** -0.5)

G_QS, G_KS, G_QD, G_KD, G_QI, G_KI = 0, 4, 6, 8, 9, 11
N_ROPE_GROUPS = 12
G_VS, G_QM = 12, 14
N_RM_GROUPS = 16

WT_ROWS = 80
VT_ROWS = 80
LOG2E = float(np.log2(np.e))

_NT = (((1,), (1,)), ((), ()))


def _cparams(n_grid):
    return pltpu.CompilerParams(
        dimension_semantics=("arbitrary",) * n_grid, vmem_limit_bytes=VMEM_LIMIT)


def _const_spec(shape):
    nd = len(shape)
    return pl.BlockSpec(shape, lambda *_: (0,) * nd, pipeline_mode=pl.Buffered(1))


def _rms_scale(x):
    return lax.rsqrt(jnp.mean(x * x, axis=-1, keepdims=True) + EPS)


def _first_half_mask():
    lane = lax.broadcasted_iota(jnp.int32, (1, LANES), 1)
    return (lane % HEAD_DIM) < (HEAD_DIM // 2), lane < HEAD_DIM


def _stack_heads(q, n_heads, low64):
    parts = []
    for h in range(n_heads):
        col = q[:, (h // 2) * LANES:(h // 2 + 1) * LANES]
        keep = low64 if h % 2 == 0 else jnp.logical_not(low64)
        parts.append(jnp.where(keep, col, jnp.zeros_like(col)))
    return jnp.concatenate(parts, axis=0)


def _memkv_kernel(mem_ref, g_ref, w_ref, k_ref, v_ref):
    m = mem_ref[0]
    mn = (m * _rms_scale(m) * g_ref[...]).astype(jnp.bfloat16)
    kv = jnp.dot(mn, w_ref[...], preferred_element_type=jnp.float32)
    k_ref[0] = kv[:, :MEM_WIDTH].astype(jnp.bfloat16)
    v_ref[0] = kv[:, MEM_WIDTH:].astype(jnp.bfloat16)


def _memkv(mem, g_mem, w_mem_kv):
    B, M, _ = mem.shape
    return pl.pallas_call(
        _memkv_kernel,
        grid=(B,),
        in_specs=[pl.BlockSpec((1, M, D_MODEL), lambda b: (b, 0, 0)),
                  _const_spec((1, D_MODEL)),
                  _const_spec((D_MODEL, 2 * MEM_WIDTH))],
        out_specs=[pl.BlockSpec((1, M, MEM_WIDTH), lambda b: (b, 0, 0)),
                   pl.BlockSpec((1, M, MEM_WIDTH), lambda b: (b, 0, 0))],
        out_shape=[jax.ShapeDtypeStruct((B, M, MEM_WIDTH), jnp.bfloat16)] * 2,
        compiler_params=_cparams(1),
        name="memkv",
    )(mem, g_mem.reshape(1, D_MODEL), w_mem_kv.astype(jnp.bfloat16))


def _proj_kernel(x_ref, pos_ref, g_ref, invf_ref, wrm_ref, wt_ref, wg_ref, bg_ref, mk_ref, mv_ref,
                 qs_ref, ks_ref, vs_ref, qd_ref, kd_ref, qi_ref, ki_ref, vdt_ref, wit_ref,
                 gates_ref, omem_ref):
    x = x_ref[0]
    h = (x * _rms_scale(x) * g_ref[...]).astype(jnp.bfloat16)
    half32, low64 = _first_half_mask()

    ang = pos_ref[0].astype(jnp.float32) * invf_ref[...]
    cos = jnp.cos(ang)
    sin = jnp.where(half32, -jnp.sin(ang), jnp.sin(ang))

    def rope(y):
        partner = jnp.where(half32, pltpu.roll(y, LANES - HEAD_DIM // 2, 1),
                            pltpu.roll(y, HEAD_DIM // 2, 1))
        return y * cos + partner * sin

    dest = {}
    for g0, (ref, n) in {G_QS: (qs_ref, 4), G_KS: (ks_ref, 2), G_QD: (qd_ref, 2), G_KD: (kd_ref, 1),
                         G_QI: (qi_ref, 2), G_KI: (ki_ref, 1), G_VS: (vs_ref, 2)}.items():
        for i in range(n):
            dest[g0 + i] = (ref, i)
    qm_pairs = {}
    for g in range(0, N_RM_GROUPS, 2):
        y2 = jnp.dot(h, wrm_ref[:, g * LANES:(g + 2) * LANES], preferred_element_type=jnp.float32)
        for u in range(2):
            y = y2[:, u * LANES:(u + 1) * LANES]
            if g + u >= G_QM:
                qm_pairs[g + u - G_QM] = y.astype(jnp.bfloat16)
                continue
            if g + u < N_ROPE_GROUPS:
                y = rope(y)
            ref, i = dest[g + u]
            ref[0, :, i * LANES:(i + 1) * LANES] = y.astype(jnp.bfloat16)

    yt = lax.dot_general(wt_ref[...], h, _NT, preferred_element_type=jnp.float32)
    vdt_ref[0, 0, :HEAD_DIM, :] = yt[:HEAD_DIM].astype(jnp.bfloat16)
    pad_row = lax.broadcasted_iota(jnp.int32, (VT_ROWS - HEAD_DIM, yt.shape[1]), 0)
    vdt_ref[0, 0, HEAD_DIM:, :] = jnp.where(pad_row == 0, 1.0, 0.0).astype(jnp.bfloat16)
    wit_ref[0] = yt[HEAD_DIM:HEAD_DIM + 8]

    n_gc = (N_BRANCH * D_MODEL) // 512
    for c in range(n_gc):
        z = jnp.dot(h, wg_ref[:, c * 512:(c + 1) * 512], preferred_element_type=jnp.float32)
        z = z + bg_ref[:, c * 512:(c + 1) * 512]
        gates_ref[0, :, c * 512:(c + 1) * 512] = jax.nn.sigmoid(z).astype(jnp.bfloat16)

    for p in range(N_MEM_HEADS // 2):
        qm = qm_pairs[p]
        mk = mk_ref[0, :, p * LANES:(p + 1) * LANES]
        mv = mv_ref[0, :, p * LANES:(p + 1) * LANES]
        o_pair = None
        for hh in range(2):
            keep = low64 if hh == 0 else jnp.logical_not(low64)
            qh = jnp.where(keep, qm, jnp.zeros_like(qm))
            s = lax.dot_general(qh, mk, _NT, preferred_element_type=jnp.float32)
            e = jnp.exp(s - jnp.max(s, axis=-1, keepdims=True))
            l = jnp.sum(e, axis=-1, keepdims=True)
            o = jnp.dot(e.astype(jnp.bfloat16), mv, preferred_element_type=jnp.float32) / l
            o_pair = o if hh == 0 else jnp.where(low64, o_pair, o)
        omem_ref[0, :, p * LANES:(p + 1) * LANES] = o_pair.astype(jnp.bfloat16)


def _proj(x, positions, g_mix, w_in, b_gate, mk, mv):
    B, S, _ = x.shape
    M = mk.shape[1]
    T = TOK_TILE
    assert S % T == 0
    sp = np.cumsum((0,) + IN_SPLITS)
    w = {n: w_in[:, sp[i]:sp[i + 1]] for i, n in enumerate(
        ("qs", "ks", "vs", "qd", "kd", "vd", "qi", "ki", "wi", "qm", "gate"))}

    def dup(a):
        return jnp.concatenate([a, a], axis=1)

    wrm = jnp.concatenate(
        [w["qs"] * ATT_SCALE,
         dup(w["ks"][:, :HEAD_DIM]), dup(w["ks"][:, HEAD_DIM:]),
         w["qd"] * (ATT_SCALE * LOG2E), dup(w["kd"]),
         w["qi"], dup(w["ki"]),
         dup(w["vs"][:, :HEAD_DIM]), dup(w["vs"][:, HEAD_DIM:]),
         w["qm"] * ATT_SCALE], axis=1).astype(jnp.bfloat16)
    assert wrm.shape[1] == N_RM_GROUPS * LANES
    wt = jnp.concatenate(
        [w["vd"].T, w["wi"].T * IDX_SCALE, jnp.zeros((WT_ROWS - HEAD_DIM - N_IDX_HEADS, D_MODEL), w_in.dtype)],
        axis=0).astype(jnp.bfloat16)
    wg = w["gate"].astype(jnp.bfloat16)
    half = HEAD_DIM // 2
    invf = jnp.power(ROPE_THETA, -jnp.arange(half, dtype=jnp.float32) / half)
    invf = jnp.tile(invf, LANES // half).reshape(1, LANES)

    tok = lambda n: pl.BlockSpec((1, T, n), lambda b, i: (b, i, 0))
    bf = jnp.bfloat16
    out_shape = [
        jax.ShapeDtypeStruct((B, S, 4 * LANES), bf),
        jax.ShapeDtypeStruct((B, S, 2 * LANES), bf),
        jax.ShapeDtypeStruct((B, S, 2 * LANES), bf),
        jax.ShapeDtypeStruct((B, S, 2 * LANES), bf),
        jax.ShapeDtypeStruct((B, S, LANES), bf),
        jax.ShapeDtypeStruct((B, S, 2 * LANES), bf),
        jax.ShapeDtypeStruct((B, S, LANES), bf),
        jax.ShapeDtypeStruct((B, S // T, VT_ROWS, T), bf),
        jax.ShapeDtypeStruct((B, 8, S), jnp.float32),
        jax.ShapeDtypeStruct((B, S, N_BRANCH * D_MODEL), bf),
        jax.ShapeDtypeStruct((B, S, MEM_WIDTH), bf),
    ]
    out_specs = [tok(4 * LANES), tok(2 * LANES), tok(2 * LANES), tok(2 * LANES), tok(LANES),
                 tok(2 * LANES), tok(LANES),
                 pl.BlockSpec((1, 1, VT_ROWS, T), lambda b, i: (b, i, 0, 0)),
                 pl.BlockSpec((1, 8, T), lambda b, i: (b, 0, i)),
                 tok(N_BRANCH * D_MODEL), tok(MEM_WIDTH)]
    return pl.pallas_call(
        _proj_kernel,
        grid=(B, S // T),
        in_specs=[tok(D_MODEL),
                  pl.BlockSpec((1, T, 1), lambda b, i: (b, i, 0)),
                  _const_spec((1, D_MODEL)), _const_spec((1, LANES)),
                  _const_spec(wrm.shape), _const_spec(wt.shape), _const_spec(wg.shape),
                  _const_spec((1, N_BRANCH * D_MODEL)),
                  pl.BlockSpec((1, M, MEM_WIDTH), lambda b, i: (b, 0, 0)),
                  pl.BlockSpec((1, M, MEM_WIDTH), lambda b, i: (b, 0, 0))],
        out_specs=out_specs,
        out_shape=out_shape,
        compiler_params=_cparams(2),
        name="proj",
    )(x, positions.reshape(B, S, 1), g_mix.reshape(1, D_MODEL), invf, wrm, wt, wg,
      b_gate.reshape(1, -1), mk, mv)


def _swa_kernel(sink_ref, q_ref, kc_ref, kp_ref, vc_ref, vp_ref, o_ref):
    i = pl.program_id(1)
    _, low64 = _first_half_mask()
    q = q_ref[0]
    kcat = jnp.concatenate([kp_ref[0], kc_ref[0]], axis=0)
    vcat = jnp.concatenate([vp_ref[0], vc_ref[0]], axis=0)
    qi = lax.broadcasted_iota(jnp.int32, (BLOCK, 2 * BLOCK), 0)
    sj = lax.broadcasted_iota(jnp.int32, (BLOCK, 2 * BLOCK), 1)
    dist = qi + BLOCK - sj
    band = (dist >= 0) & (dist < WINDOW)
    G = N_SWA_HEADS // N_SWA_KV
    for bq in range(SWA_TILE // BLOCK):
        valid = band
        if bq == 0:
            valid = band & ((sj + jnp.where(i > 0, BLOCK, 0)) >= BLOCK)
        qb = q[bq * BLOCK:(bq + 1) * BLOCK]
        for g in range(N_SWA_KV):
            qst = _stack_heads(qb[:, g * G * HEAD_DIM:(g + 1) * G * HEAD_DIM], G, low64)
            kw = kcat[bq * BLOCK:(bq + 2) * BLOCK, g * LANES:(g + 1) * LANES]
            vw = vcat[bq * BLOCK:(bq + 2) * BLOCK, g * LANES:(g + 1) * LANES]
            s = lax.dot_general(qst, kw, _NT, preferred_element_type=jnp.float32)
            es, ls = [], []
            for hh in range(G):
                sh = jnp.where(valid, s[hh * BLOCK:(hh + 1) * BLOCK], -jnp.inf)
                sink = sink_ref[g * G + hh]
                m = jnp.maximum(jnp.max(sh, axis=-1, keepdims=True), sink)
                e = jnp.exp(sh - m)
                ls.append(jnp.sum(e, axis=-1, keepdims=True) + jnp.exp(sink - m))
                es.append(e.astype(jnp.bfloat16))
            o = jnp.dot(jnp.concatenate(es, axis=0), vw, preferred_element_type=jnp.float32)
            for pp in range(G // 2):
                o0 = o[(2 * pp) * BLOCK:(2 * pp + 1) * BLOCK] / ls[2 * pp]
                o1 = o[(2 * pp + 1) * BLOCK:(2 * pp + 2) * BLOCK] / ls[2 * pp + 1]
                c0 = (g * (G // 2) + pp) * LANES
                o_ref[0, bq * BLOCK:(bq + 1) * BLOCK, c0:c0 + LANES] = (
                    jnp.where(low64, o0, o1).astype(jnp.bfloat16))


def _swa(qs, ks, vs, sinks):
    B, S, _ = qs.shape
    TQ = SWA_TILE
    r = TQ // BLOCK
    cur = lambda n: pl.BlockSpec((1, TQ, n), lambda b, i: (b, i, 0))
    prev = lambda n: pl.BlockSpec((1, BLOCK, n), lambda b, i: (b, jnp.maximum(i * r - 1, 0), 0))
    return pl.pallas_call(
        _swa_kernel,
        grid=(B, S // TQ),
        in_specs=[pl.BlockSpec(memory_space=pltpu.SMEM),
                  cur(4 * LANES), cur(2 * LANES), prev(2 * LANES), cur(2 * LANES), prev(2 * LANES)],
        out_specs=cur(SWA_WIDTH),
        out_shape=jax.ShapeDtypeStruct((B, S, SWA_WIDTH), jnp.bfloat16),
        compiler_params=_cparams(2),
        name="swa",
    )(sinks, qs, ks, ks, vs, vs)


def _dsa_kernel(qi_ref, wi_ref, qd_ref, ki_ref, kd_ref, vt_ref, ltri_ref, o_ref, sc_ref, sta_ref, stb_ref,
                *, top_k):
    j = pl.program_id(1)
    QW = DSA_Q
    nt = ((j + 1) * QW + KEY_TILE - 1) // KEY_TILE
    _, low64 = _first_half_mask()
    lane = lax.broadcasted_iota(jnp.int32, (1, QW), 1)
    qpos = j * QW + lane
    w = wi_ref[0]
    q_idx = _stack_heads(qi_ref[0], N_IDX_HEADS, low64)
    q_att = _stack_heads(qd_ref[0], N_DSA_HEADS, low64)
    kf = jnp.float32(top_k)
    row = lax.broadcasted_iota(jnp.int32, (KEY_TILE, QW), 0)
    n_acc = 64

    def fold(a, op):
        a = a.reshape(KEY_TILE // n_acc, n_acc, QW)
        r = a[0]
        for u in range(1, KEY_TILE // n_acc):
            r = op(r, a[u])
        return r

    def idx_product_stage(t, st_ref):
        ks = ki_ref[0, pl.ds(t * KEY_TILE, KEY_TILE), :]
        st_ref[...] = lax.dot_general(ks, q_idx, _NT, preferred_element_type=jnp.float32)

    def idx_combine_stage(t, carry, st_ref, diagonal):
        isc = None
        for h in range(N_IDX_HEADS):
            term = jnp.maximum(st_ref[:, h * QW:(h + 1) * QW], 0.0) * w[h:h + 1, :]
            isc = term if isc is None else isc + term
        lo_src = isc
        if diagonal:
            valid = row <= (qpos - t * KEY_TILE)
            lo_src = jnp.where(valid, isc, jnp.inf)
            isc = jnp.where(valid, isc, -jnp.inf)
        sc_ref[pl.ds(t * KEY_TILE, KEY_TILE), :] = isc
        mn, mx, cge, cgt = carry
        return (jnp.minimum(mn, fold(lo_src, jnp.minimum)),
                jnp.maximum(mx, fold(isc, jnp.maximum)),
                cge + fold((isc >= 0.0).astype(jnp.float32), jnp.add),
                cgt + fold((isc > 0.0).astype(jnp.float32), jnp.add))

    def idx_pair_body(u, carry):
        t = 2 * u
        idx_product_stage(t + 1, stb_ref)
        carry = idx_combine_stage(t, carry, sta_ref, False)
        idx_product_stage(t + 2, sta_ref)
        return idx_combine_stage(t + 1, carry, stb_ref, False)

    n_pairs = (nt - 1) // 2
    t_rem = 2 * n_pairs
    inf_acc = jnp.full((n_acc, QW), jnp.inf, jnp.float32)
    zero_acc = jnp.zeros((n_acc, QW), jnp.float32)
    idx_product_stage(0, sta_ref)
    carry = lax.fori_loop(0, n_pairs, idx_pair_body, (inf_acc, -inf_acc, zero_acc, zero_acc))

    def idx_last_two():
        idx_product_stage(t_rem + 1, stb_ref)
        return idx_combine_stage(t_rem + 1, idx_combine_stage(t_rem, carry, sta_ref, False), stb_ref, True)

    mn, mx, cge, cgt = lax.cond(nt - t_rem == 2, idx_last_two,
                                lambda: idx_combine_stage(t_rem, carry, sta_ref, True))
    smin = jnp.min(mn, axis=0, keepdims=True)
    smax = jnp.max(mx, axis=0, keepdims=True)
    t_first = jnp.min(mx, axis=0, keepdims=True)
    cge0 = jnp.sum(cge, axis=0, keepdims=True)
    cgt0 = jnp.sum(cgt, axis=0, keepdims=True)

    def count_ge(thr):
        def body(t, acc):
            blk = sc_ref[pl.ds(t * KEY_TILE, KEY_TILE), :]
            return acc + fold((blk >= thr).astype(jnp.float32), jnp.add)
        acc = lax.fori_loop(0, nt, body, jnp.zeros((n_acc, QW), jnp.float32))
        return jnp.sum(acc, axis=0, keepdims=True)

    def top_below(bound):
        n_set = 4

        def body(t, accs):
            blk = sc_ref[pl.ds(t * KEY_TILE, KEY_TILE), :].reshape(KEY_TILE // 8, 8, QW)
            accs = [list(a) for a in accs]
            for v in range(KEY_TILE // 8):
                x = jnp.where(blk[v] < bound, blk[v], -jnp.inf)
                lv = accs[v % n_set]
                for i in range(TOP_R):
                    hi_ = jnp.maximum(lv[i], x)
                    if i < TOP_R - 1:
                        x = jnp.minimum(lv[i], x)
                    lv[i] = hi_
            return tuple(tuple(a) for a in accs)

        ninf = jnp.full((8, QW), -jnp.inf, jnp.float32)
        accs = lax.fori_loop(0, nt, body, tuple(tuple(ninf for _ in range(TOP_R)) for _ in range(n_set)))
        cand = jnp.concatenate([a for s_ in accs for a in s_], axis=0)
        ridx = lax.broadcasted_iota(jnp.int32, cand.shape, 0).astype(jnp.float32)
        tops = []
        for _ in range(TOP_R):
            e = jnp.max(cand, axis=0, keepdims=True)
            first = jnp.min(jnp.where(cand == e, ridx, float(cand.shape[0])), axis=0, keepdims=True)
            cand = jnp.where(ridx == first, -jnp.inf, cand)
            tops.append(e)
        return tops

    def all_set(flag):
        return jnp.min(flag) > 0.5

    inf_row = jnp.full((1, QW), jnp.inf, jnp.float32)

    def select_threshold():
        one = jnp.ones((1, QW), jnp.float32)
        zero = jnp.zeros((1, QW), jnp.float32)
        k_aim = kf + AIM_OFFSET

        def logc(c):
            return jnp.log(jnp.maximum(c, 0.5) / k_aim)

        n_valid = (qpos + 1).astype(jnp.float32)
        pos_k = cgt0 >= kf
        neg_k = cge0 < kf
        st = dict(
            lo=jnp.where(pos_k, 0.0, smin), flo=jnp.where(pos_k, logc(cgt0), logc(n_valid)),
            bhi=jnp.where(neg_k, 0.0, smax), fhi=jnp.where(neg_k, logc(cge0), logc(one)),
            hx=jnp.where(neg_k, 0.0, inf_row), chx=jnp.where(neg_k, cge0, zero),
            side=zero,
            found=jnp.where(pos_k | neg_k, zero, one),
            thr=zero, keep=jnp.where(pos_k | neg_k, inf_row, kf - cgt0))

        def probe(st, mid):
            c = count_ge(mid)
            hit = (c == kf) & (st["found"] < 0.5)
            ge = c >= kf
            f = logc(c)
            newside = jnp.where(ge, one, -one)
            same = newside == st["side"]
            fhi = jnp.where(ge & same, st["fhi"] * 0.5, st["fhi"])
            flo = jnp.where(jnp.logical_not(ge) & same, st["flo"] * 0.5, st["flo"])
            return dict(
                lo=jnp.where(ge, mid, st["lo"]), flo=jnp.where(ge, f, flo),
                bhi=jnp.where(ge, st["bhi"], mid), fhi=jnp.where(ge, fhi, f),
                hx=jnp.where(ge, st["hx"], mid), chx=jnp.where(ge, st["chx"], c),
                side=newside,
                found=jnp.where(hit, one, st["found"]),
                thr=jnp.where(hit, mid, st["thr"]), keep=st["keep"])

        def next_mid(st):
            lo, bhi = st["lo"], st["bhi"]
            den = st["flo"] - st["fhi"]
            mid = lo + (bhi - lo) * (st["flo"] / jnp.where(den == 0.0, 1.0, den))
            inside = (mid > lo) & (mid < bhi)
            return jnp.where(inside, mid, lo + (bhi - lo) * 0.5)

        def resolve_top(st):
            tops = top_below(st["hx"])
            r = kf - st["chx"]
            tr = tops[TOP_R - 1]
            for i in range(TOP_R - 2, -1, -1):
                tr = jnp.where(r == float(i + 1), tops[i], tr)
            n_gt = zero
            for i in range(TOP_R - 1):
                n_gt = n_gt + (tops[i] > tr).astype(jnp.float32)
            ok = (st["found"] < 0.5) & (r <= float(TOP_R))
            st = dict(st, found=jnp.where(ok, one, st["found"]), thr=jnp.where(ok, tr, st["thr"]),
                      keep=jnp.where(ok, r - n_gt, st["keep"]))
            return st, tops[0]

        first = jnp.minimum(jnp.maximum(t_first, smin), smax)
        inside = (first > st["lo"]) & (first < st["bhi"])
        st = probe(st, jnp.where(inside, first, next_mid(st)))
        st = dict(st, side=zero)
        st = lax.fori_loop(0, N_INTERP, lambda _, s_: probe(s_, next_mid(s_)), st)
        st, _ = resolve_top(st)

        def fb_body(st):
            st = probe(st, next_mid(st))
            st, u = resolve_top(st)
            cu = count_ge(u)
            ok = (st["found"] < 0.5) & (cu >= kf)
            go = (st["found"] < 0.5) & (cu < kf)
            return dict(
                lo=st["lo"], flo=st["flo"],
                bhi=jnp.where(go, jnp.minimum(st["bhi"], u), st["bhi"]),
                fhi=jnp.where(go, logc(cu), st["fhi"]),
                hx=jnp.where(go, u, st["hx"]), chx=jnp.where(go, cu, st["chx"]),
                side=st["side"],
                found=jnp.where(ok, one, st["found"]),
                thr=jnp.where(ok, u, st["thr"]), keep=jnp.where(ok, kf - st["chx"], st["keep"]))

        st = lax.while_loop(lambda s_: jnp.logical_not(all_set(s_["found"])), fb_body, st)
        return st["thr"], st["keep"]

    thr, keep = lax.cond((j + 1) * QW > top_k, select_threshold,
                         lambda: (jnp.full((1, QW), -F32_MAX, jnp.float32), inf_row))

    NH = N_DSA_HEADS

    def score_stage(t, n_tie, st_ref):
        kd = kd_ref[0, pl.ds(t * KEY_TILE, KEY_TILE), :]
        sd = lax.dot_general(kd, q_att, _NT, preferred_element_type=jnp.float32)
        blk = sc_ref[pl.ds(t * KEY_TILE, KEY_TILE), :]
        tie = blk == thr
        tie_bf = tie.astype(jnp.float32).astype(jnp.bfloat16)
        ranks = []
        for sb in range(KEY_TILE // BLOCK):
            rk = n_tie + jnp.dot(ltri_ref[...], tie_bf[sb * BLOCK:(sb + 1) * BLOCK],
                                 preferred_element_type=jnp.float32)
            n_tie = rk[BLOCK - 1:BLOCK, :]
            ranks.append(rk)
        sel = (blk > thr) | (tie & (jnp.concatenate(ranks, axis=0) <= keep))
        cms = []
        for h in range(NH):
            sdm = jnp.where(sel, sd[:, h * QW:(h + 1) * QW], -jnp.inf)
            st_ref[:, h * QW:(h + 1) * QW] = sdm
            cms.append(jnp.max(sdm, axis=0, keepdims=True))
        return jnp.concatenate(cms, axis=1), n_tie

    def value_stage(t, m, acc, cmax, st_ref):
        m_new = jnp.maximum(m, cmax)
        p = jnp.exp2(st_ref[...] - m_new).astype(jnp.bfloat16)
        pv = jnp.dot(vt_ref[0, t], p, preferred_element_type=jnp.float32)
        return m_new, acc * jnp.exp2(m - m_new) + pv

    def pair_body(u, carry):
        m, acc, cmax, n_tie = carry
        t = 2 * u
        cmax_b, n_tie = score_stage(t + 1, n_tie, stb_ref)
        m, acc = value_stage(t, m, acc, cmax, sta_ref)
        cmax, n_tie = score_stage(t + 2, n_tie, sta_ref)
        m, acc = value_stage(t + 1, m, acc, cmax_b, stb_ref)
        return m, acc, cmax, n_tie

    m0 = jnp.full((1, NH * QW), NEG_BIG, jnp.float32)
    a0 = jnp.zeros((VT_ROWS, NH * QW), jnp.float32)
    cmax, n_tie = score_stage(0, jnp.zeros((1, QW), jnp.float32), sta_ref)
    n_pairs = (nt - 1) // 2
    m, acc, cmax, n_tie = lax.fori_loop(0, n_pairs, pair_body, (m0, a0, cmax, n_tie))
    t_rem = 2 * n_pairs

    def last_two():
        cmax_l, _ = score_stage(t_rem + 1, n_tie, stb_ref)
        m_, acc_ = value_stage(t_rem, m, acc, cmax, sta_ref)
        return value_stage(t_rem + 1, m_, acc_, cmax_l, stb_ref)[1]

    acc = lax.cond(nt - t_rem == 2, last_two, lambda: value_stage(t_rem, m, acc, cmax, sta_ref)[1])
    ot = acc[:HEAD_DIM] / acc[HEAD_DIM:HEAD_DIM + 1]
    for p in range(NH // 2):
        pair = jnp.concatenate([ot[:, (2 * p) * QW:(2 * p + 1) * QW],
                                ot[:, (2 * p + 1) * QW:(2 * p + 2) * QW]], axis=0)
        o_ref[0, :, p * LANES:(p + 1) * LANES] = pair.T.astype(jnp.bfloat16)


def _dsa(qi, wit, qd, ki, kd, vdt, top_k):
    B, S, _ = qi.shape
    assert S % KEY_TILE == 0 and KEY_TILE % BLOCK == 0 and S % DSA_Q == 0 and DSA_Q % LANES == 0
    assert top_k % DSA_Q == 0, "a grid step must not mix queries with <= top_k and > top_k valid keys"
    blk = lambda n: pl.BlockSpec((1, DSA_Q, n), lambda b, j: (b, j, 0))
    whole = lambda n: pl.BlockSpec((1, S, n), lambda b, j: (b, 0, 0))
    ltri = jnp.tril(jnp.ones((BLOCK, BLOCK), jnp.bfloat16))
    return pl.pallas_call(
        functools.partial(_dsa_kernel, top_k=top_k),
        grid=(B, S // DSA_Q),
        in_specs=[blk(2 * LANES),
                  pl.BlockSpec((1, 8, DSA_Q), lambda b, j: (b, 0, j)),
                  blk(2 * LANES), whole(LANES), whole(LANES),
                  pl.BlockSpec((1, S // KEY_TILE, VT_ROWS, KEY_TILE), lambda b, j: (b, 0, 0, 0)),
                  _const_spec(ltri.shape)],
        out_specs=blk(DSA_WIDTH),
        out_shape=jax.ShapeDtypeStruct((B, S, DSA_WIDTH), jnp.bfloat16),
        scratch_shapes=[pltpu.VMEM((S, DSA_Q), jnp.float32),
                        pltpu.VMEM((KEY_TILE, N_DSA_HEADS * DSA_Q), jnp.float32),
                        pltpu.VMEM((KEY_TILE, N_DSA_HEADS * DSA_Q), jnp.float32)],
        compiler_params=_cparams(2),
        name="dsa",
    )(qi, wit, qd, ki, kd, vdt, ltri)


def _merge_kernel(x_ref, gates_ref, os_ref, od_ref, om_ref, wps_ref, wpd_ref, wpm_ref, wo_ref,
                  gm_ref, w1_ref, w2_ref, gf_ref, out_ref):
    f32 = jnp.float32
    x = x_ref[...]
    merged = None
    for b, (o_ref, w_ref) in enumerate(((os_ref, wps_ref), (od_ref, wpd_ref), (om_ref, wpm_ref))):
        y = jnp.dot(o_ref[...], w_ref[...], preferred_element_type=f32)
        y = y * gates_ref[:, b * D_MODEL:(b + 1) * D_MODEL].astype(f32)
        merged = y if merged is None else merged + y
    x1 = x + jnp.dot(merged.astype(jnp.bfloat16), wo_ref[...], preferred_element_type=f32)
    h = (x1 * _rms_scale(x1) * gm_ref[...]).astype(jnp.bfloat16)
    acc = x1
    FC = 1024
    for c in range(D_FF // FC):
        hid = jnp.maximum(jnp.dot(h, w1_ref[:, c * FC:(c + 1) * FC], preferred_element_type=f32), 0.0)
        hid = (hid * hid).astype(jnp.bfloat16)
        acc = acc + jnp.dot(hid, w2_ref[c * FC:(c + 1) * FC, :], preferred_element_type=f32)
    out_ref[...] = acc * _rms_scale(acc) * gf_ref[...]


def _merge(x2d, gates, o_swa, o_dsa, o_mem, w_proj_swa, w_proj_dsa, w_proj_mem, w_out,
           g_mlp, w_mlp_in, w_mlp_out, g_final):
    N = x2d.shape[0]
    T = TOK_TILE
    bf = jnp.bfloat16
    tok = lambda n: pl.BlockSpec((T, n), lambda i: (i, 0))
    ws = [w_proj_swa.astype(bf), w_proj_dsa.astype(bf), w_proj_mem.astype(bf), w_out.astype(bf)]
    w1, w2 = w_mlp_in.astype(bf), w_mlp_out.astype(bf)
    return pl.pallas_call(
        _merge_kernel,
        grid=(N // T,),
        in_specs=[tok(D_MODEL), tok(N_BRANCH * D_MODEL), tok(SWA_WIDTH), tok(DSA_WIDTH), tok(MEM_WIDTH)]
                 + [_const_spec(a.shape) for a in ws]
                 + [_const_spec((1, D_MODEL)), _const_spec(w1.shape), _const_spec(w2.shape),
                    _const_spec((1, D_MODEL))],
        out_specs=tok(D_MODEL),
        out_shape=jax.ShapeDtypeStruct((N, D_MODEL), jnp.float32),
        compiler_params=_cparams(1),
        name="merge",
    )(x2d, gates, o_swa, o_dsa, o_mem, *ws, g_mlp.reshape(1, D_MODEL), w1, w2,
      g_final.reshape(1, D_MODEL))


def kernel(x, mem, positions, g_mix, w_in, b_gate, sinks, g_mem, w_mem_kv, w_proj_swa, w_proj_dsa,
           w_proj_mem, w_out, g_mlp, w_mlp_in, w_mlp_out, g_final):
    B, S, D = x.shape
    assert g_mix.shape[0] == 1, "the final norm is fused into the single layer's merge kernel"
    top_k = min(TOPK_MAX, S // 4)
    for l in range(1):
        mk, mv = _memkv(mem, g_mem[l], w_mem_kv[l])
        (qs, ks, vs, qd, kd, qi, ki, vdt, wit, gates, o_mem) = _proj(
            x, positions, g_mix[l], w_in[l], b_gate[l], mk, mv)
        o_swa = _swa(qs, ks, vs, sinks[l])
        o_dsa = _dsa(qi, wit, qd, ki, kd, vdt, top_k)
        x = _merge(x.reshape(B * S, D), gates.reshape(B * S, -1), o_swa.reshape(B * S, -1),
                   o_dsa.reshape(B * S, -1), o_mem.reshape(B * S, -1), w_proj_swa[l], w_proj_dsa[l],
                   w_proj_mem[l], w_out[l], g_mlp[l], w_mlp_in[l], w_mlp_out[l], g_final).reshape(B, S, D)
    return x
```

```python
import functools

import jax
import jax.numpy as jnp
import numpy as np
from jax import lax
from jax.experimental import pallas as pl
from jax.experimental.pallas import tpu as pltpu

D_MODEL = 1024
HEAD_DIM = 64
N_SWA_HEADS = 8
N_SWA_KV = 2
WINDOW = 128
BLOCK = 128
N_DSA_HEADS = 4
N_IDX_HEADS = 4
IDX_DIM = 64
TOPK_MAX = 256
N_MEM_HEADS = 4
N_BRANCH = 3
D_FF = 4 * D_MODEL
ROPE_THETA = 10000.0
EPS = 1e-6

SWA_WIDTH = N_SWA_HEADS * HEAD_DIM
SWA_KV_WIDTH = N_SWA_KV * HEAD_DIM
DSA_WIDTH = N_DSA_HEADS * HEAD_DIM
MEM_WIDTH = N_MEM_HEADS * HEAD_DIM
IN_SPLITS = (SWA_WIDTH, SWA_KV_WIDTH, SWA_KV_WIDTH,
             DSA_WIDTH, HEAD_DIM, HEAD_DIM,
             N_IDX_HEADS * IDX_DIM, IDX_DIM, N_IDX_HEADS,
             MEM_WIDTH, N_BRANCH * D_MODEL)

LANES = 128
TOK_TILE = 512
PROJ_TILE = 1024
KEY_TILE = 512
SWA_TILE = 1024
DSA_Q = 256
VMEM_LIMIT = 56 * 1024 * 1024
NEG_BIG = -1e30
F32_MAX = float(np.finfo(np.float32).max)
N_INTERP = 6
TOP_R = 4
AIM_OFFSET = -2.5

ATT_SCALE = HEAD_DIM ** -0.5
IDX_SCALE = (N_IDX_HEADS ** -0.5) * (IDX_DIM ** -0.5)

G_QS, G_KS, G_QD, G_KD, G_QI, G_KI = 0, 4, 6, 8, 9, 11
N_ROPE_GROUPS = 12
G_VS, G_QM = 12, 14
N_RM_GROUPS = 16

WT_ROWS = 80
VT_ROWS = 80
LOG2E = float(np.log2(np.e))

_NT = (((1,), (1,)), ((), ()))


def _cparams(n_grid):
    return pltpu.CompilerParams(
        dimension_semantics=("arbitrary",) * n_grid, vmem_limit_bytes=VMEM_LIMIT)


def _const_spec(shape):
    nd = len(shape)
    return pl.BlockSpec(shape, lambda *_: (0,) * nd, pipeline_mode=pl.Buffered(1))


def _rms_scale(x):
    return lax.rsqrt(jnp.mean(x * x, axis=-1, keepdims=True) + EPS)


def _first_half_mask():
    lane = lax.broadcasted_iota(jnp.int32, (1, LANES), 1)
    return (lane % HEAD_DIM) < (HEAD_DIM // 2), lane < HEAD_DIM


def _stack_heads(q, n_heads, low64):
    parts = []
    for h in range(n_heads):
        col = q[:, (h // 2) * LANES:(h // 2 + 1) * LANES]
        keep = low64 if h % 2 == 0 else jnp.logical_not(low64)
        parts.append(jnp.where(keep, col, jnp.zeros_like(col)))
    return jnp.concatenate(parts, axis=0)


def _memkv_kernel(mem_ref, g_ref, w_ref, k_ref, v_ref):
    m = mem_ref[0]
    mn = (m * _rms_scale(m) * g_ref[...]).astype(jnp.bfloat16)
    kv = jnp.dot(mn, w_ref[...], preferred_element_type=jnp.float32)
    k_ref[0] = kv[:, :MEM_WIDTH].astype(jnp.bfloat16)
    v_ref[0] = kv[:, MEM_WIDTH:].astype(jnp.bfloat16)


def _memkv(mem, g_mem, w_mem_kv):
    B, M, _ = mem.shape
    return pl.pallas_call(
        _memkv_kernel,
        grid=(B,),
        in_specs=[pl.BlockSpec((1, M, D_MODEL), lambda b: (b, 0, 0)),
                  _const_spec((1, D_MODEL)),
                  _const_spec((D_MODEL, 2 * MEM_WIDTH))],
        out_specs=[pl.BlockSpec((1, M, MEM_WIDTH), lambda b: (b, 0, 0)),
                   pl.BlockSpec((1, M, MEM_WIDTH), lambda b: (b, 0, 0))],
        out_shape=[jax.ShapeDtypeStruct((B, M, MEM_WIDTH), jnp.bfloat16)] * 2,
        compiler_params=_cparams(1),
        name="memkv",
    )(mem, g_mem.reshape(1, D_MODEL), w_mem_kv.astype(jnp.bfloat16))


def _proj_kernel(x_ref, pos_ref, g_ref, invf_ref, wrm_ref, wt_ref, wg_ref, bg_ref, mk_ref, mv_ref,
                 qs_ref, ks_ref, vs_ref, qd_ref, kd_ref, qi_ref, ki_ref, vdt_ref, wit_ref,
                 gates_ref, omem_ref):
    x = x_ref[0]
    h = (x * _rms_scale(x) * g_ref[...]).astype(jnp.bfloat16)
    half32, low64 = _first_half_mask()

    ang = pos_ref[0].astype(jnp.float32) * invf_ref[...]
    cos = jnp.cos(ang)
    sin = jnp.where(half32, -jnp.sin(ang), jnp.sin(ang))

    def rope(y):
        partner = jnp.where(half32, pltpu.roll(y, LANES - HEAD_DIM // 2, 1),
                            pltpu.roll(y, HEAD_DIM // 2, 1))
        return y * cos + partner * sin

    dest = {}
    for g0, (ref, n) in {G_QS: (qs_ref, 4), G_KS: (ks_ref, 2), G_QD: (qd_ref, 2), G_KD: (kd_ref, 1),
                         G_QI: (qi_ref, 2), G_KI: (ki_ref, 1), G_VS: (vs_ref, 2)}.items():
        for i in range(n):
            dest[g0 + i] = (ref, i)
    def group_pair(g):
        y2 = jnp.dot(h, wrm_ref[:, g * LANES:(g + 2) * LANES], preferred_element_type=jnp.float32)
        return [y2[:, u * LANES:(u + 1) * LANES] for u in range(2)]

    def rope_pair(g):
        for u, y in enumerate(group_pair(g)):
            if g + u < N_ROPE_GROUPS:
                y = rope(y)
            ref, i = dest[g + u]
            ref[0, :, i * LANES:(i + 1) * LANES] = y.astype(jnp.bfloat16)

    def gate_chunk(c):
        z = jnp.dot(h, wg_ref[:, c * 512:(c + 1) * 512], preferred_element_type=jnp.float32)
        z = z + bg_ref[:, c * 512:(c + 1) * 512]
        gates_ref[0, :, c * 512:(c + 1) * 512] = jax.nn.sigmoid(z).astype(jnp.bfloat16)

    mem_parts = []
    for p, qm in enumerate(group_pair(G_QM)):
        qm = qm.astype(jnp.bfloat16)
        mk = mk_ref[0, :, p * LANES:(p + 1) * LANES]
        for hh in range(2):
            keep = low64 if hh == 0 else jnp.logical_not(low64)
            qh = jnp.where(keep, qm, jnp.zeros_like(qm))
            s = lax.dot_general(qh, mk, _NT, preferred_element_type=jnp.float32)
            e = jnp.exp(s - jnp.max(s, axis=-1, keepdims=True))
            mem_parts.append((e.astype(jnp.bfloat16), jnp.sum(e, axis=-1, keepdims=True)))

    n_gc = (N_BRANCH * D_MODEL) // 512
    for c in range(max(n_gc, G_QM // 2)):
        if c < G_QM // 2:
            rope_pair(2 * c)
        if c < n_gc:
            gate_chunk(c)

    yt = lax.dot_general(wt_ref[...], h, _NT, preferred_element_type=jnp.float32)
    pad_row = lax.broadcasted_iota(jnp.int32, (VT_ROWS - HEAD_DIM, KEY_TILE), 0)
    ones_row = jnp.where(pad_row == 0, 1.0, 0.0).astype(jnp.bfloat16)
    for u in range(vdt_ref.shape[1]):
        vdt_ref[0, u, :HEAD_DIM, :] = yt[:HEAD_DIM, u * KEY_TILE:(u + 1) * KEY_TILE].astype(jnp.bfloat16)
        vdt_ref[0, u, HEAD_DIM:, :] = ones_row
    wit_ref[0] = yt[HEAD_DIM:HEAD_DIM + 8]

    for p in range(N_MEM_HEADS // 2):
        mv = mv_ref[0, :, p * LANES:(p + 1) * LANES]
        o_pair = None
        for hh in range(2):
            e, l = mem_parts[2 * p + hh]
            o = jnp.dot(e, mv, preferred_element_type=jnp.float32) / l
            o_pair = o if hh == 0 else jnp.where(low64, o_pair, o)
        omem_ref[0, :, p * LANES:(p + 1) * LANES] = o_pair.astype(jnp.bfloat16)


def _proj(x, positions, g_mix, w_in, b_gate, mk, mv):
    B, S, _ = x.shape
    M = mk.shape[1]
    T = PROJ_TILE
    assert S % T == 0 and T % KEY_TILE == 0
    sp = np.cumsum((0,) + IN_SPLITS)
    w = {n: w_in[:, sp[i]:sp[i + 1]] for i, n in enumerate(
        ("qs", "ks", "vs", "qd", "kd", "vd", "qi", "ki", "wi", "qm", "gate"))}

    def dup(a):
        return jnp.concatenate([a, a], axis=1)

    wrm = jnp.concatenate(
        [w["qs"] * ATT_SCALE,
         dup(w["ks"][:, :HEAD_DIM]), dup(w["ks"][:, HEAD_DIM:]),
         w["qd"] * (ATT_SCALE * LOG2E), dup(w["kd"]),
         w["qi"], dup(w["ki"]),
         dup(w["vs"][:, :HEAD_DIM]), dup(w["vs"][:, HEAD_DIM:]),
         w["qm"] * ATT_SCALE], axis=1).astype(jnp.bfloat16)
    assert wrm.shape[1] == N_RM_GROUPS * LANES
    wt = jnp.concatenate(
        [w["vd"].T, w["wi"].T * IDX_SCALE, jnp.zeros((WT_ROWS - HEAD_DIM - N_IDX_HEADS, D_MODEL), w_in.dtype)],
        axis=0).astype(jnp.bfloat16)
    wg = w["gate"].astype(jnp.bfloat16)
    half = HEAD_DIM // 2
    invf = jnp.power(ROPE_THETA, -jnp.arange(half, dtype=jnp.float32) / half)
    invf = jnp.tile(invf, LANES // half).reshape(1, LANES)

    tok = lambda n: pl.BlockSpec((1, T, n), lambda b, i: (b, i, 0))
    bf = jnp.bfloat16
    out_shape = [
        jax.ShapeDtypeStruct((B, S, 4 * LANES), bf),
        jax.ShapeDtypeStruct((B, S, 2 * LANES), bf),
        jax.ShapeDtypeStruct((B, S, 2 * LANES), bf),
        jax.ShapeDtypeStruct((B, S, 2 * LANES), bf),
        jax.ShapeDtypeStruct((B, S, LANES), bf),
        jax.ShapeDtypeStruct((B, S, 2 * LANES), bf),
        jax.ShapeDtypeStruct((B, S, LANES), bf),
        jax.ShapeDtypeStruct((B, S // KEY_TILE, VT_ROWS, KEY_TILE), bf),
        jax.ShapeDtypeStruct((B, 8, S), jnp.float32),
        jax.ShapeDtypeStruct((B, S, N_BRANCH * D_MODEL), bf),
        jax.ShapeDtypeStruct((B, S, MEM_WIDTH), bf),
    ]
    out_specs = [tok(4 * LANES), tok(2 * LANES), tok(2 * LANES), tok(2 * LANES), tok(LANES),
                 tok(2 * LANES), tok(LANES),
                 pl.BlockSpec((1, T // KEY_TILE, VT_ROWS, KEY_TILE), lambda b, i: (b, i, 0, 0)),
                 pl.BlockSpec((1, 8, T), lambda b, i: (b, 0, i)),
                 tok(N_BRANCH * D_MODEL), tok(MEM_WIDTH)]
    return pl.pallas_call(
        _proj_kernel,
        grid=(B, S // T),
        in_specs=[tok(D_MODEL),
                  pl.BlockSpec((1, T, 1), lambda b, i: (b, i, 0)),
                  _const_spec((1, D_MODEL)), _const_spec((1, LANES)),
                  _const_spec(wrm.shape), _const_spec(wt.shape), _const_spec(wg.shape),
                  _const_spec((1, N_BRANCH * D_MODEL)),
                  pl.BlockSpec((1, M, MEM_WIDTH), lambda b, i: (b, 0, 0)),
                  pl.BlockSpec((1, M, MEM_WIDTH), lambda b, i: (b, 0, 0))],
        out_specs=out_specs,
        out_shape=out_shape,
        compiler_params=_cparams(2),
        name="proj",
    )(x, positions.reshape(B, S, 1), g_mix.reshape(1, D_MODEL), invf, wrm, wt, wg,
      b_gate.reshape(1, -1), mk, mv)


def _swa_kernel(sink_ref, q_ref, kc_ref, kp_ref, vc_ref, vp_ref, o_ref):
    i = pl.program_id(1)
    _, low64 = _first_half_mask()
    q = q_ref[0]
    kcat = jnp.concatenate([kp_ref[0], kc_ref[0]], axis=0)
    vcat = jnp.concatenate([vp_ref[0], vc_ref[0]], axis=0)
    qi = lax.broadcasted_iota(jnp.int32, (BLOCK, 2 * BLOCK), 0)
    sj = lax.broadcasted_iota(jnp.int32, (BLOCK, 2 * BLOCK), 1)
    dist = qi + BLOCK - sj
    band = (dist >= 0) & (dist < WINDOW)
    G = N_SWA_HEADS // N_SWA_KV
    for bq in range(SWA_TILE // BLOCK):
        valid = band
        if bq == 0:
            valid = band & ((sj + jnp.where(i > 0, BLOCK, 0)) >= BLOCK)
        qb = q[bq * BLOCK:(bq + 1) * BLOCK]
        for g in range(N_SWA_KV):
            qst = _stack_heads(qb[:, g * G * HEAD_DIM:(g + 1) * G * HEAD_DIM], G, low64)
            kw = kcat[bq * BLOCK:(bq + 2) * BLOCK, g * LANES:(g + 1) * LANES]
            vw = vcat[bq * BLOCK:(bq + 2) * BLOCK, g * LANES:(g + 1) * LANES]
            s = lax.dot_general(qst, kw, _NT, preferred_element_type=jnp.float32)
            es, ls = [], []
            for hh in range(G):
                sh = jnp.where(valid, s[hh * BLOCK:(hh + 1) * BLOCK], -jnp.inf)
                sink = sink_ref[g * G + hh]
                m = jnp.maximum(jnp.max(sh, axis=-1, keepdims=True), sink)
                e = jnp.exp(sh - m)
                ls.append(jnp.sum(e, axis=-1, keepdims=True) + jnp.exp(sink - m))
                es.append(e.astype(jnp.bfloat16))
            o = jnp.dot(jnp.concatenate(es, axis=0), vw, preferred_element_type=jnp.float32)
            for pp in range(G // 2):
                o0 = o[(2 * pp) * BLOCK:(2 * pp + 1) * BLOCK] / ls[2 * pp]
                o1 = o[(2 * pp + 1) * BLOCK:(2 * pp + 2) * BLOCK] / ls[2 * pp + 1]
                c0 = (g * (G // 2) + pp) * LANES
                o_ref[0, bq * BLOCK:(bq + 1) * BLOCK, c0:c0 + LANES] = (
                    jnp.where(low64, o0, o1).astype(jnp.bfloat16))


def _swa(qs, ks, vs, sinks):
    B, S, _ = qs.shape
    TQ = SWA_TILE
    r = TQ // BLOCK
    cur = lambda n: pl.BlockSpec((1, TQ, n), lambda b, i: (b, i, 0))
    prev = lambda n: pl.BlockSpec((1, BLOCK, n), lambda b, i: (b, jnp.maximum(i * r - 1, 0), 0))
    return pl.pallas_call(
        _swa_kernel,
        grid=(B, S // TQ),
        in_specs=[pl.BlockSpec(memory_space=pltpu.SMEM),
                  cur(4 * LANES), cur(2 * LANES), prev(2 * LANES), cur(2 * LANES), prev(2 * LANES)],
        out_specs=cur(SWA_WIDTH),
        out_shape=jax.ShapeDtypeStruct((B, S, SWA_WIDTH), jnp.bfloat16),
        compiler_params=_cparams(2),
        name="swa",
    )(sinks, qs, ks, ks, vs, vs)


def _dsa_kernel(qi_ref, wi_ref, qd_ref, ki_ref, kd_ref, vt_ref, ltri_ref, o_ref, sc_ref, sta_ref, stb_ref,
                *, top_k):
    j = pl.program_id(1)
    QW = DSA_Q
    nt = ((j + 1) * QW + KEY_TILE - 1) // KEY_TILE
    _, low64 = _first_half_mask()
    lane = lax.broadcasted_iota(jnp.int32, (1, QW), 1)
    qpos = j * QW + lane
    w = wi_ref[0]
    q_idx = _stack_heads(qi_ref[0], N_IDX_HEADS, low64)
    q_att = _stack_heads(qd_ref[0], N_DSA_HEADS, low64)
    kf = jnp.float32(top_k)
    row = lax.broadcasted_iota(jnp.int32, (KEY_TILE, QW), 0)
    n_acc = 64

    def fold(a, op):
        a = a.reshape(KEY_TILE // n_acc, n_acc, QW)
        r = a[0]
        for u in range(1, KEY_TILE // n_acc):
            r = op(r, a[u])
        return r

    def idx_product_stage(t, st_ref):
        ks = ki_ref[0, pl.ds(t * KEY_TILE, KEY_TILE), :]
        st_ref[...] = lax.dot_general(ks, q_idx, _NT, preferred_element_type=jnp.float32)

    def idx_combine_stage(t, carry, st_ref, diagonal):
        isc = None
        for h in range(N_IDX_HEADS):
            term = jnp.maximum(st_ref[:, h * QW:(h + 1) * QW], 0.0) * w[h:h + 1, :]
            isc = term if isc is None else isc + term
        lo_src = isc
        if diagonal:
            valid = row <= (qpos - t * KEY_TILE)
            lo_src = jnp.where(valid, isc, jnp.inf)
            isc = jnp.where(valid, isc, -jnp.inf)
        sc_ref[pl.ds(t * KEY_TILE, KEY_TILE), :] = isc
        mn, mx, cge, cgt = carry
        return (jnp.minimum(mn, fold(lo_src, jnp.minimum)),
                jnp.maximum(mx, fold(isc, jnp.maximum)),
                cge + fold((isc >= 0.0).astype(jnp.float32), jnp.add),
                cgt + fold((isc > 0.0).astype(jnp.float32), jnp.add))

    def idx_pair_body(u, carry):
        t = 2 * u
        idx_product_stage(t + 1, stb_ref)
        carry = idx_combine_stage(t, carry, sta_ref, False)
        idx_product_stage(t + 2, sta_ref)
        return idx_combine_stage(t + 1, carry, stb_ref, False)

    n_pairs = (nt - 1) // 2
    t_rem = 2 * n_pairs
    inf_acc = jnp.full((n_acc, QW), jnp.inf, jnp.float32)
    zero_acc = jnp.zeros((n_acc, QW), jnp.float32)
    idx_product_stage(0, sta_ref)
    carry = lax.fori_loop(0, n_pairs, idx_pair_body, (inf_acc, -inf_acc, zero_acc, zero_acc))

    def idx_last_two():
        idx_product_stage(t_rem + 1, stb_ref)
        return idx_combine_stage(t_rem + 1, idx_combine_stage(t_rem, carry, sta_ref, False), stb_ref, True)

    mn, mx, cge, cgt = lax.cond(nt - t_rem == 2, idx_last_two,
                                lambda: idx_combine_stage(t_rem, carry, sta_ref, True))
    smin = jnp.min(mn, axis=0, keepdims=True)
    smax = jnp.max(mx, axis=0, keepdims=True)
    t_first = jnp.min(mx, axis=0, keepdims=True)
    cge0 = jnp.sum(cge, axis=0, keepdims=True)
    cgt0 = jnp.sum(cgt, axis=0, keepdims=True)

    def count_ge(thr):
        def body(t, acc):
            for u in range(KEY_TILE // n_acc):
                blk = sc_ref[pl.ds(t * KEY_TILE + u * n_acc, n_acc), :]
                acc = acc + (blk >= thr).astype(jnp.float32)
            return acc
        acc = lax.fori_loop(0, nt, body, jnp.zeros((n_acc, QW), jnp.float32))
        return jnp.sum(acc, axis=0, keepdims=True)

    def top_below(bound):
        n_set = 4

        def body(t, accs):
            blk = sc_ref[pl.ds(t * KEY_TILE, KEY_TILE), :].reshape(KEY_TILE // 8, 8, QW)
            accs = [list(a) for a in accs]
            for v in range(KEY_TILE // 8):
                x = jnp.where(blk[v] < bound, blk[v], -jnp.inf)
                lv = accs[v % n_set]
                for i in range(TOP_R):
                    hi_ = jnp.maximum(lv[i], x)
                    if i < TOP_R - 1:
                        x = jnp.minimum(lv[i], x)
                    lv[i] = hi_
            return tuple(tuple(a) for a in accs)

        ninf = jnp.full((8, QW), -jnp.inf, jnp.float32)
        accs = lax.fori_loop(0, nt, body, tuple(tuple(ninf for _ in range(TOP_R)) for _ in range(n_set)))
        cand = jnp.concatenate([a for s_ in accs for a in s_], axis=0)
        ridx = lax.broadcasted_iota(jnp.int32, cand.shape, 0).astype(jnp.float32)
        tops = []
        for _ in range(TOP_R):
            e = jnp.max(cand, axis=0, keepdims=True)
            first = jnp.min(jnp.where(cand == e, ridx, float(cand.shape[0])), axis=0, keepdims=True)
            cand = jnp.where(ridx == first, -jnp.inf, cand)
            tops.append(e)
        return tops

    def all_set(flag):
        return jnp.min(flag) > 0.5

    inf_row = jnp.full((1, QW), jnp.inf, jnp.float32)

    def select_threshold():
        one = jnp.ones((1, QW), jnp.float32)
        zero = jnp.zeros((1, QW), jnp.float32)
        k_aim = kf + AIM_OFFSET

        def logc(c):
            return jnp.log(jnp.maximum(c, 0.5) / k_aim)

        n_valid = (qpos + 1).astype(jnp.float32)
        pos_k = cgt0 >= kf
        neg_k = cge0 < kf
        st = dict(
            lo=jnp.where(pos_k, 0.0, smin), flo=jnp.where(pos_k, logc(cgt0), logc(n_valid)),
            bhi=jnp.where(neg_k, 0.0, smax), fhi=jnp.where(neg_k, logc(cge0), logc(one)),
            hx=jnp.where(neg_k, 0.0, inf_row), chx=jnp.where(neg_k, cge0, zero),
            side=zero,
            found=jnp.where(pos_k | neg_k, zero, one),
            thr=zero, keep=jnp.where(pos_k | neg_k, inf_row, kf - cgt0))

        def probe(st, mid):
            c = count_ge(mid)
            hit = (c == kf) & (st["found"] < 0.5)
            ge = c >= kf
            f = logc(c)
            newside = jnp.where(ge, one, -one)
            same = newside == st["side"]
            fhi = jnp.where(ge & same, st["fhi"] * 0.5, st["fhi"])
            flo = jnp.where(jnp.logical_not(ge) & same, st["flo"] * 0.5, st["flo"])
            return dict(
                lo=jnp.where(ge, mid, st["lo"]), flo=jnp.where(ge, f, flo),
                bhi=jnp.where(ge, st["bhi"], mid), fhi=jnp.where(ge, fhi, f),
                hx=jnp.where(ge, st["hx"], mid), chx=jnp.where(ge, st["chx"], c),
                side=newside,
                found=jnp.where(hit, one, st["found"]),
                thr=jnp.where(hit, mid, st["thr"]), keep=st["keep"])

        def next_mid(st):
            lo, bhi = st["lo"], st["bhi"]
            den = st["flo"] - st["fhi"]
            mid = lo + (bhi - lo) * (st["flo"] / jnp.where(den == 0.0, 1.0, den))
            inside = (mid > lo) & (mid < bhi)
            return jnp.where(inside, mid, lo + (bhi - lo) * 0.5)

        def resolve_top(st):
            tops = top_below(st["hx"])
            r = kf - st["chx"]
            tr = tops[TOP_R - 1]
            for i in range(TOP_R - 2, -1, -1):
                tr = jnp.where(r == float(i + 1), tops[i], tr)
            n_gt = zero
            for i in range(TOP_R - 1):
                n_gt = n_gt + (tops[i] > tr).astype(jnp.float32)
            ok = (st["found"] < 0.5) & (r <= float(TOP_R))
            st = dict(st, found=jnp.where(ok, one, st["found"]), thr=jnp.where(ok, tr, st["thr"]),
                      keep=jnp.where(ok, r - n_gt, st["keep"]))
            return st, tops[0]

        first = jnp.minimum(jnp.maximum(t_first, smin), smax)
        inside = (first > st["lo"]) & (first < st["bhi"])
        st = probe(st, jnp.where(inside, first, next_mid(st)))
        st = dict(st, side=zero)
        st = lax.fori_loop(0, N_INTERP, lambda _, s_: probe(s_, next_mid(s_)), st)
        st, _ = resolve_top(st)

        def fb_body(carry):
            it, st = carry
            st = probe(st, next_mid(st))
            st, u = resolve_top(st)
            cu = count_ge(u)
            ok = (st["found"] < 0.5) & (cu >= kf)
            go = (st["found"] < 0.5) & (cu < kf)
            return it + 1, dict(
                lo=st["lo"], flo=st["flo"],
                bhi=jnp.where(go, jnp.minimum(st["bhi"], u), st["bhi"]),
                fhi=jnp.where(go, logc(cu), st["fhi"]),
                hx=jnp.where(go, u, st["hx"]), chx=jnp.where(go, cu, st["chx"]),
                side=st["side"],
                found=jnp.where(ok, one, st["found"]),
                thr=jnp.where(ok, u, st["thr"]), keep=jnp.where(ok, kf - st["chx"], st["keep"]))

        _, st = lax.while_loop(
            lambda c: jnp.logical_not(all_set(c[1]["found"])) & (c[0] <= nt * KEY_TILE),
            fb_body, (jnp.int32(0), st))
        return st["thr"], st["keep"]

    thr, keep = lax.cond((j + 1) * QW > top_k, select_threshold,
                         lambda: (jnp.full((1, QW), -F32_MAX, jnp.float32), inf_row))

    NH = N_DSA_HEADS

    def score_stage(t, n_tie, st_ref):
        kd = kd_ref[0, pl.ds(t * KEY_TILE, KEY_TILE), :]
        blk = sc_ref[pl.ds(t * KEY_TILE, KEY_TILE), :]
        tie = blk == thr
        tie_bf = tie.astype(jnp.float32).astype(jnp.bfloat16)
        ranks = []
        for sb in range(KEY_TILE // BLOCK):
            rk = n_tie + jnp.dot(ltri_ref[...], tie_bf[sb * BLOCK:(sb + 1) * BLOCK],
                                 preferred_element_type=jnp.float32)
            n_tie = rk[BLOCK - 1:BLOCK, :]
            ranks.append(rk)
        sel = (blk > thr) | (tie & (jnp.concatenate(ranks, axis=0) <= keep))
        cms = []
        hpd = max(1, 2 * LANES // QW)
        for h0 in range(0, NH, hpd):
            sd = lax.dot_general(kd, q_att[h0 * QW:(h0 + hpd) * QW], _NT,
                                 preferred_element_type=jnp.float32)
            for hh in range(hpd):
                h = h0 + hh
                sdm = jnp.where(sel, sd[:, hh * QW:(hh + 1) * QW], -jnp.inf)
                st_ref[:, h * QW:(h + 1) * QW] = sdm
                cms.append(jnp.max(sdm, axis=0, keepdims=True))
        return jnp.concatenate(cms, axis=1), n_tie

    def value_stage(t, m, acc, cmax, st_ref):
        m_new = jnp.maximum(m, cmax)
        p = jnp.exp2(st_ref[...] - m_new).astype(jnp.bfloat16)
        pv = jnp.dot(vt_ref[0, t], p, preferred_element_type=jnp.float32)
        return m_new, acc * jnp.exp2(m - m_new) + pv

    def pair_body(u, carry):
        m, acc, cmax, n_tie = carry
        t = 2 * u
        cmax_b, n_tie = score_stage(t + 1, n_tie, stb_ref)
        m, acc = value_stage(t, m, acc, cmax, sta_ref)
        cmax, n_tie = score_stage(t + 2, n_tie, sta_ref)
        m, acc = value_stage(t + 1, m, acc, cmax_b, stb_ref)
        return m, acc, cmax, n_tie

    m0 = jnp.full((1, NH * QW), NEG_BIG, jnp.float32)
    a0 = jnp.zeros((VT_ROWS, NH * QW), jnp.float32)
    cmax, n_tie = score_stage(0, jnp.zeros((1, QW), jnp.float32), sta_ref)
    n_pairs = (nt - 1) // 2
    m, acc, cmax, n_tie = lax.fori_loop(0, n_pairs, pair_body, (m0, a0, cmax, n_tie))
    t_rem = 2 * n_pairs

    def last_two():
        cmax_l, _ = score_stage(t_rem + 1, n_tie, stb_ref)
        m_, acc_ = value_stage(t_rem, m, acc, cmax, sta_ref)
        return value_stage(t_rem + 1, m_, acc_, cmax_l, stb_ref)[1]

    acc = lax.cond(nt - t_rem == 2, last_two, lambda: value_stage(t_rem, m, acc, cmax, sta_ref)[1])
    ot = acc[:HEAD_DIM] / acc[HEAD_DIM:HEAD_DIM + 1]
    for p in range(NH // 2):
        pair = jnp.concatenate([ot[:, (2 * p) * QW:(2 * p + 1) * QW],
                                ot[:, (2 * p + 1) * QW:(2 * p + 2) * QW]], axis=0)
        o_ref[0, :, p * LANES:(p + 1) * LANES] = pair.T.astype(jnp.bfloat16)


def _dsa(qi, wit, qd, ki, kd, vdt, top_k):
    B, S, _ = qi.shape
    assert S % KEY_TILE == 0 and KEY_TILE % BLOCK == 0 and S % DSA_Q == 0 and DSA_Q % LANES == 0
    assert top_k % DSA_Q == 0, "a grid step must not mix queries with <= top_k and > top_k valid keys"
    blk = lambda n: pl.BlockSpec((1, DSA_Q, n), lambda b, j: (b, j, 0))
    whole = lambda n: pl.BlockSpec((1, S, n), lambda b, j: (b, 0, 0))
    ltri = jnp.tril(jnp.ones((BLOCK, BLOCK), jnp.bfloat16))
    return pl.pallas_call(
        functools.partial(_dsa_kernel, top_k=top_k),
        grid=(B, S // DSA_Q),
        in_specs=[blk(2 * LANES),
                  pl.BlockSpec((1, 8, DSA_Q), lambda b, j: (b, 0, j)),
                  blk(2 * LANES), whole(LANES), whole(LANES),
                  pl.BlockSpec((1, S // KEY_TILE, VT_ROWS, KEY_TILE), lambda b, j: (b, 0, 0, 0)),
                  _const_spec(ltri.shape)],
        out_specs=blk(DSA_WIDTH),
        out_shape=jax.ShapeDtypeStruct((B, S, DSA_WIDTH), jnp.bfloat16),
        scratch_shapes=[pltpu.VMEM((S, DSA_Q), jnp.float32),
                        pltpu.VMEM((KEY_TILE, N_DSA_HEADS * DSA_Q), jnp.float32),
                        pltpu.VMEM((KEY_TILE, N_DSA_HEADS * DSA_Q), jnp.float32)],
        compiler_params=_cparams(2),
        name="dsa",
    )(qi, wit, qd, ki, kd, vdt, ltri)


def _merge_kernel(x_ref, gates_ref, os_ref, od_ref, om_ref, wps_ref, wpd_ref, wpm_ref, wo_ref,
                  gm_ref, w1_ref, w2_ref, gf_ref, out_ref):
    f32 = jnp.float32
    x = x_ref[...]
    merged = None
    for b, (o_ref, w_ref) in enumerate(((os_ref, wps_ref), (od_ref, wpd_ref), (om_ref, wpm_ref))):
        y = jnp.dot(o_ref[...], w_ref[...], preferred_element_type=f32)
        y = y * gates_ref[:, b * D_MODEL:(b + 1) * D_MODEL].astype(f32)
        merged = y if merged is None else merged + y
    x1 = x + jnp.dot(merged.astype(jnp.bfloat16), wo_ref[...], preferred_element_type=f32)
    h = (x1 * _rms_scale(x1) * gm_ref[...]).astype(jnp.bfloat16)
    acc = x1
    FC = 1024
    for c in range(D_FF // FC):
        hid = jnp.maximum(jnp.dot(h, w1_ref[:, c * FC:(c + 1) * FC], preferred_element_type=f32), 0.0)
        hid = (hid * hid).astype(jnp.bfloat16)
        acc = acc + jnp.dot(hid, w2_ref[c * FC:(c + 1) * FC, :], preferred_element_type=f32)
    out_ref[...] = acc * _rms_scale(acc) * gf_ref[...]


def _merge(x2d, gates, o_swa, o_dsa, o_mem, w_proj_swa, w_proj_dsa, w_proj_mem, w_out,
           g_mlp, w_mlp_in, w_mlp_out, g_final):
    N = x2d.shape[0]
    T = TOK_TILE
    bf = jnp.bfloat16
    tok = lambda n: pl.BlockSpec((T, n), lambda i: (i, 0))
    ws = [w_proj_swa.astype(bf), w_proj_dsa.astype(bf), w_proj_mem.astype(bf), w_out.astype(bf)]
    w1, w2 = w_mlp_in.astype(bf), w_mlp_out.astype(bf)
    return pl.pallas_call(
        _merge_kernel,
        grid=(N // T,),
        in_specs=[tok(D_MODEL), tok(N_BRANCH * D_MODEL), tok(SWA_WIDTH), tok(DSA_WIDTH), tok(MEM_WIDTH)]
                 + [_const_spec(a.shape) for a in ws]
                 + [_const_spec((1, D_MODEL)), _const_spec(w1.shape), _const_spec(w2.shape),
                    _const_spec((1, D_MODEL))],
        out_specs=tok(D_MODEL),
        out_shape=jax.ShapeDtypeStruct((N, D_MODEL), jnp.float32),
        compiler_params=_cparams(1),
        name="merge",
    )(x2d, gates, o_swa, o_dsa, o_mem, *ws, g_mlp.reshape(1, D_MODEL), w1, w2,
      g_final.reshape(1, D_MODEL))


def kernel(x, mem, positions, g_mix, w_in, b_gate, sinks, g_mem, w_mem_kv, w_proj_swa, w_proj_dsa,
           w_proj_mem, w_out, g_mlp, w_mlp_in, w_mlp_out, g_final):
    B, S, D = x.shape
    assert g_mix.shape[0] == 1, "the final norm is fused into the single layer's merge kernel"
    top_k = min(TOPK_MAX, S // 4)
    for l in range(1):
        mk, mv = _memkv(mem, g_mem[l], w_mem_kv[l])
        (qs, ks, vs, qd, kd, qi, ki, vdt, wit, gates, o_mem) = _proj(
            x, positions, g_mix[l], w_in[l], b_gate[l], mk, mv)
        o_swa = _swa(qs, ks, vs, sinks[l])
        o_dsa = _dsa(qi, wit, qd, ki, kd, vdt, top_k)
        x = _merge(x.reshape(B * S, D), gates.reshape(B * S, -1), o_swa.reshape(B * S, -1),
                   o_dsa.reshape(B * S, -1), o_mem.reshape(B * S, -1), w_proj_swa[l], w_proj_dsa[l],
                   w_proj_mem[l], w_out[l], g_mlp[l], w_mlp_in[l], w_mlp_out[l], g_final).reshape(B, S, D)
    return x
```

```python
import functools

import jax
import jax.numpy as jnp
import numpy as np
from jax import lax
from jax.experimental import pallas as pl
from jax.experimental.pallas import tpu as pltpu

D_MODEL = 1024
HEAD_DIM = 64
N_SWA_HEADS = 8
N_SWA_KV = 2
WINDOW = 128
BLOCK = 128
N_DSA_HEADS = 4
N_IDX_HEADS = 4
IDX_DIM = 64
TOPK_MAX = 256
N_MEM_HEADS = 4
N_BRANCH = 3
D_FF = 4 * D_MODEL
ROPE_THETA = 10000.0
EPS = 1e-6

SWA_WIDTH = N_SWA_HEADS * HEAD_DIM
SWA_KV_WIDTH = N_SWA_KV * HEAD_DIM
DSA_WIDTH = N_DSA_HEADS * HEAD_DIM
MEM_WIDTH = N_MEM_HEADS * HEAD_DIM
IN_SPLITS = (SWA_WIDTH, SWA_KV_WIDTH, SWA_KV_WIDTH,
             DSA_WIDTH, HEAD_DIM, HEAD_DIM,
             N_IDX_HEADS * IDX_DIM, IDX_DIM, N_IDX_HEADS,
             MEM_WIDTH, N_BRANCH * D_MODEL)

LANES = 128
TOK_TILE = 512
PROJ_TILE = 1024
KEY_TILE = 512
SWA_TILE = 2048
DSA_Q = 256
VMEM_LIMIT = 56 * 1024 * 1024
NEG_BIG = -1e30
F32_MAX = float(np.finfo(np.float32).max)
N_INTERP = 6
TOP_R = 4
AIM_OFFSET = -2.5

ATT_SCALE = HEAD_DIM ** -0.5
IDX_SCALE = (N_IDX_HEADS ** -0.5) * (IDX_DIM ** -0.5)

G_QS, G_KS, G_QD, G_KD, G_QI, G_KI = 0, 4, 6, 8, 9, 11
N_ROPE_GROUPS = 12
G_VS, G_QM = 12, 14
N_RM_GROUPS = 16

WT_ROWS = 80
VT_ROWS = 80
LOG2E = float(np.log2(np.e))

_NT = (((1,), (1,)), ((), ()))


def _cparams(n_grid):
    return pltpu.CompilerParams(
        dimension_semantics=("arbitrary",) * n_grid, vmem_limit_bytes=VMEM_LIMIT)


def _const_spec(shape):
    nd = len(shape)
    return pl.BlockSpec(shape, lambda *_: (0,) * nd, pipeline_mode=pl.Buffered(1))


def _rms_scale(x):
    return lax.rsqrt(jnp.mean(x * x, axis=-1, keepdims=True) + EPS)


def _first_half_mask():
    lane = lax.broadcasted_iota(jnp.int32, (1, LANES), 1)
    return (lane % HEAD_DIM) < (HEAD_DIM // 2), lane < HEAD_DIM


def _stack_heads(q, n_heads, low64):
    parts = []
    for h in range(n_heads):
        col = q[:, (h // 2) * LANES:(h // 2 + 1) * LANES]
        keep = low64 if h % 2 == 0 else jnp.logical_not(low64)
        parts.append(jnp.where(keep, col, jnp.zeros_like(col)))
    return jnp.concatenate(parts, axis=0)


def _memkv_kernel(mem_ref, g_ref, w_ref, k_ref, v_ref):
    m = mem_ref[0]
    mn = (m * _rms_scale(m) * g_ref[...]).astype(jnp.bfloat16)
    kv = jnp.dot(mn, w_ref[...], preferred_element_type=jnp.float32)
    k_ref[0] = kv[:, :MEM_WIDTH].astype(jnp.bfloat16)
    v_ref[0] = kv[:, MEM_WIDTH:].astype(jnp.bfloat16)


def _memkv(mem, g_mem, w_mem_kv):
    B, M, _ = mem.shape
    return pl.pallas_call(
        _memkv_kernel,
        grid=(B,),
        in_specs=[pl.BlockSpec((1, M, D_MODEL), lambda b: (b, 0, 0)),
                  _const_spec((1, D_MODEL)),
                  _const_spec((D_MODEL, 2 * MEM_WIDTH))],
        out_specs=[pl.BlockSpec((1, M, MEM_WIDTH), lambda b: (b, 0, 0)),
                   pl.BlockSpec((1, M, MEM_WIDTH), lambda b: (b, 0, 0))],
        out_shape=[jax.ShapeDtypeStruct((B, M, MEM_WIDTH), jnp.bfloat16)] * 2,
        compiler_params=_cparams(1),
        name="memkv",
    )(mem, g_mem.reshape(1, D_MODEL), w_mem_kv.astype(jnp.bfloat16))


def _proj_kernel(x_ref, pos_ref, g_ref, invf_ref, wrm_ref, wt_ref, wg_ref, bg_ref, mk_ref, mv_ref,
                 qs_ref, ks_ref, vs_ref, qd_ref, kd_ref, qi_ref, ki_ref, vdt_ref, wit_ref,
                 gates_ref, omem_ref):
    x = x_ref[0]
    h = (x * _rms_scale(x) * g_ref[...]).astype(jnp.bfloat16)
    half32, low64 = _first_half_mask()

    ang = pos_ref[0].astype(jnp.float32) * invf_ref[...]
    cos = jnp.cos(ang)
    sin = jnp.where(half32, -jnp.sin(ang), jnp.sin(ang))

    def rope(y):
        partner = jnp.where(half32, pltpu.roll(y, LANES - HEAD_DIM // 2, 1),
                            pltpu.roll(y, HEAD_DIM // 2, 1))
        return y * cos + partner * sin

    dest = {}
    for g0, (ref, n) in {G_QS: (qs_ref, 4), G_KS: (ks_ref, 2), G_QD: (qd_ref, 2), G_KD: (kd_ref, 1),
                         G_QI: (qi_ref, 2), G_KI: (ki_ref, 1), G_VS: (vs_ref, 2)}.items():
        for i in range(n):
            dest[g0 + i] = (ref, i)
    def group_pair(g):
        y2 = jnp.dot(h, wrm_ref[:, g * LANES:(g + 2) * LANES], preferred_element_type=jnp.float32)
        return [y2[:, u * LANES:(u + 1) * LANES] for u in range(2)]

    def rope_pair(g):
        for u, y in enumerate(group_pair(g)):
            if g + u < N_ROPE_GROUPS:
                y = rope(y)
            ref, i = dest[g + u]
            ref[0, :, i * LANES:(i + 1) * LANES] = y.astype(jnp.bfloat16)

    def gate_chunk(c):
        z = jnp.dot(h, wg_ref[:, c * 512:(c + 1) * 512], preferred_element_type=jnp.float32)
        z = z + bg_ref[:, c * 512:(c + 1) * 512]
        gates_ref[0, :, c * 512:(c + 1) * 512] = jax.nn.sigmoid(z).astype(jnp.bfloat16)

    mem_parts = []
    for p, qm in enumerate(group_pair(G_QM)):
        qm = qm.astype(jnp.bfloat16)
        mk = mk_ref[0, :, p * LANES:(p + 1) * LANES]
        for hh in range(2):
            keep = low64 if hh == 0 else jnp.logical_not(low64)
            qh = jnp.where(keep, qm, jnp.zeros_like(qm))
            s = lax.dot_general(qh, mk, _NT, preferred_element_type=jnp.float32)
            e = jnp.exp2(s - jnp.max(s, axis=-1, keepdims=True))
            mem_parts.append((e.astype(jnp.bfloat16), jnp.sum(e, axis=-1, keepdims=True)))

    n_gc = (N_BRANCH * D_MODEL) // 512
    for c in range(max(n_gc, G_QM // 2)):
        if c < G_QM // 2:
            rope_pair(2 * c)
        if c < n_gc:
            gate_chunk(c)

    yt = lax.dot_general(wt_ref[...], h, _NT, preferred_element_type=jnp.float32)
    pad_row = lax.broadcasted_iota(jnp.int32, (VT_ROWS - HEAD_DIM, KEY_TILE), 0)
    ones_row = jnp.where(pad_row == 0, 1.0, 0.0).astype(jnp.bfloat16)
    for u in range(vdt_ref.shape[1]):
        vdt_ref[0, u, :HEAD_DIM, :] = yt[:HEAD_DIM, u * KEY_TILE:(u + 1) * KEY_TILE].astype(jnp.bfloat16)
        vdt_ref[0, u, HEAD_DIM:, :] = ones_row
    wit_ref[0] = yt[HEAD_DIM:HEAD_DIM + 8]

    for p in range(N_MEM_HEADS // 2):
        mv = mv_ref[0, :, p * LANES:(p + 1) * LANES]
        o_pair = None
        for hh in range(2):
            e, l = mem_parts[2 * p + hh]
            o = jnp.dot(e, mv, preferred_element_type=jnp.float32) / l
            o_pair = o if hh == 0 else jnp.where(low64, o_pair, o)
        omem_ref[0, :, p * LANES:(p + 1) * LANES] = o_pair.astype(jnp.bfloat16)


def _proj(x, positions, g_mix, w_in, b_gate, mk, mv):
    B, S, _ = x.shape
    M = mk.shape[1]
    T = PROJ_TILE
    assert S % T == 0 and T % KEY_TILE == 0
    sp = np.cumsum((0,) + IN_SPLITS)
    w = {n: w_in[:, sp[i]:sp[i + 1]] for i, n in enumerate(
        ("qs", "ks", "vs", "qd", "kd", "vd", "qi", "ki", "wi", "qm", "gate"))}

    def dup(a):
        return jnp.concatenate([a, a], axis=1)

    wrm = jnp.concatenate(
        [w["qs"] * (ATT_SCALE * LOG2E),
         dup(w["ks"][:, :HEAD_DIM]), dup(w["ks"][:, HEAD_DIM:]),
         w["qd"] * (ATT_SCALE * LOG2E), dup(w["kd"]),
         w["qi"], dup(w["ki"]),
         dup(w["vs"][:, :HEAD_DIM]), dup(w["vs"][:, HEAD_DIM:]),
         w["qm"] * (ATT_SCALE * LOG2E)], axis=1).astype(jnp.bfloat16)
    assert wrm.shape[1] == N_RM_GROUPS * LANES
    wt = jnp.concatenate(
        [w["vd"].T, w["wi"].T * IDX_SCALE, jnp.zeros((WT_ROWS - HEAD_DIM - N_IDX_HEADS, D_MODEL), w_in.dtype)],
        axis=0).astype(jnp.bfloat16)
    wg = w["gate"].astype(jnp.bfloat16)
    half = HEAD_DIM // 2
    invf = jnp.power(ROPE_THETA, -jnp.arange(half, dtype=jnp.float32) / half)
    invf = jnp.tile(invf, LANES // half).reshape(1, LANES)

    tok = lambda n: pl.BlockSpec((1, T, n), lambda b, i: (b, i, 0))
    bf = jnp.bfloat16
    out_shape = [
        jax.ShapeDtypeStruct((B, S, 4 * LANES), bf),
        jax.ShapeDtypeStruct((B, S, 2 * LANES), bf),
        jax.ShapeDtypeStruct((B, S, 2 * LANES), bf),
        jax.ShapeDtypeStruct((B, S, 2 * LANES), bf),
        jax.ShapeDtypeStruct((B, S, LANES), bf),
        jax.ShapeDtypeStruct((B, S, 2 * LANES), bf),
        jax.ShapeDtypeStruct((B, S, LANES), bf),
        jax.ShapeDtypeStruct((B, S // KEY_TILE, VT_ROWS, KEY_TILE), bf),
        jax.ShapeDtypeStruct((B, 8, S), jnp.float32),
        jax.ShapeDtypeStruct((B, S, N_BRANCH * D_MODEL), bf),
        jax.ShapeDtypeStruct((B, S, MEM_WIDTH), bf),
    ]
    out_specs = [tok(4 * LANES), tok(2 * LANES), tok(2 * LANES), tok(2 * LANES), tok(LANES),
                 tok(2 * LANES), tok(LANES),
                 pl.BlockSpec((1, T // KEY_TILE, VT_ROWS, KEY_TILE), lambda b, i: (b, i, 0, 0)),
                 pl.BlockSpec((1, 8, T), lambda b, i: (b, 0, i)),
                 tok(N_BRANCH * D_MODEL), tok(MEM_WIDTH)]
    return pl.pallas_call(
        _proj_kernel,
        grid=(B, S // T),
        in_specs=[tok(D_MODEL),
                  pl.BlockSpec((1, T, 1), lambda b, i: (b, i, 0)),
                  _const_spec((1, D_MODEL)), _const_spec((1, LANES)),
                  _const_spec(wrm.shape), _const_spec(wt.shape), _const_spec(wg.shape),
                  _const_spec((1, N_BRANCH * D_MODEL)),
                  pl.BlockSpec((1, M, MEM_WIDTH), lambda b, i: (b, 0, 0)),
                  pl.BlockSpec((1, M, MEM_WIDTH), lambda b, i: (b, 0, 0))],
        out_specs=out_specs,
        out_shape=out_shape,
        compiler_params=_cparams(2),
        name="proj",
    )(x, positions.reshape(B, S, 1), g_mix.reshape(1, D_MODEL), invf, wrm, wt, wg,
      b_gate.reshape(1, -1), mk, mv)


def _swa_kernel(sink_ref, q_ref, kc_ref, kp_ref, vc_ref, vp_ref, o_ref):
    i = pl.program_id(1)
    _, low64 = _first_half_mask()
    q = q_ref[0]
    kcat = jnp.concatenate([kp_ref[0], kc_ref[0]], axis=0)
    vcat = jnp.concatenate([vp_ref[0], vc_ref[0]], axis=0)
    qi = lax.broadcasted_iota(jnp.int32, (BLOCK, 2 * BLOCK), 0)
    sj = lax.broadcasted_iota(jnp.int32, (BLOCK, 2 * BLOCK), 1)
    dist = qi + BLOCK - sj
    band = (dist >= 0) & (dist < WINDOW)
    G = N_SWA_HEADS // N_SWA_KV
    for bq in range(SWA_TILE // BLOCK):
        valid = band
        if bq == 0:
            valid = band & ((sj + jnp.where(i > 0, BLOCK, 0)) >= BLOCK)
        qb = q[bq * BLOCK:(bq + 1) * BLOCK]
        for g in range(N_SWA_KV):
            qst = _stack_heads(qb[:, g * G * HEAD_DIM:(g + 1) * G * HEAD_DIM], G, low64)
            kw = kcat[bq * BLOCK:(bq + 2) * BLOCK, g * LANES:(g + 1) * LANES]
            vw = vcat[bq * BLOCK:(bq + 2) * BLOCK, g * LANES:(g + 1) * LANES]
            s = lax.dot_general(qst, kw, _NT, preferred_element_type=jnp.float32)
            es, ls = [], []
            for hh in range(G):
                sh = jnp.where(valid, s[hh * BLOCK:(hh + 1) * BLOCK], -jnp.inf)
                sink = sink_ref[g * G + hh] * LOG2E
                m = jnp.maximum(jnp.max(sh, axis=-1, keepdims=True), sink)
                e = jnp.exp2(sh - m)
                ls.append(jnp.sum(e, axis=-1, keepdims=True) + jnp.exp2(sink - m))
                es.append(e.astype(jnp.bfloat16))
            o = jnp.dot(jnp.concatenate(es, axis=0), vw, preferred_element_type=jnp.float32)
            for pp in range(G // 2):
                o0 = o[(2 * pp) * BLOCK:(2 * pp + 1) * BLOCK] / ls[2 * pp]
                o1 = o[(2 * pp + 1) * BLOCK:(2 * pp + 2) * BLOCK] / ls[2 * pp + 1]
                c0 = (g * (G // 2) + pp) * LANES
                o_ref[0, bq * BLOCK:(bq + 1) * BLOCK, c0:c0 + LANES] = (
                    jnp.where(low64, o0, o1).astype(jnp.bfloat16))


def _swa(qs, ks, vs, sinks):
    B, S, _ = qs.shape
    TQ = SWA_TILE
    assert S % TQ == 0
    r = TQ // BLOCK
    cur = lambda n: pl.BlockSpec((1, TQ, n), lambda b, i: (b, i, 0))
    prev = lambda n: pl.BlockSpec((1, BLOCK, n), lambda b, i: (b, jnp.maximum(i * r - 1, 0), 0))
    return pl.pallas_call(
        _swa_kernel,
        grid=(B, S // TQ),
        in_specs=[pl.BlockSpec(memory_space=pltpu.SMEM),
                  cur(4 * LANES), cur(2 * LANES), prev(2 * LANES), cur(2 * LANES), prev(2 * LANES)],
        out_specs=cur(SWA_WIDTH),
        out_shape=jax.ShapeDtypeStruct((B, S, SWA_WIDTH), jnp.bfloat16),
        compiler_params=_cparams(2),
        name="swa",
    )(sinks, qs, ks, ks, vs, vs)


def _dsa_kernel(qi_ref, wi_ref, qd_ref, ki_ref, kd_ref, vt_ref, ltri_ref, o_ref, sc_ref, sta_ref, stb_ref,
                *, top_k):
    j = pl.program_id(1)
    QW = DSA_Q
    nt = ((j + 1) * QW + KEY_TILE - 1) // KEY_TILE
    _, low64 = _first_half_mask()
    lane = lax.broadcasted_iota(jnp.int32, (1, QW), 1)
    qpos = j * QW + lane
    w = wi_ref[0]
    q_idx = _stack_heads(qi_ref[0], N_IDX_HEADS, low64)
    q_att = _stack_heads(qd_ref[0], N_DSA_HEADS, low64)
    kf = jnp.float32(top_k)
    row = lax.broadcasted_iota(jnp.int32, (KEY_TILE, QW), 0)
    n_acc = 64
    n_cnt = 32

    def fold(a, op):
        a = a.reshape(KEY_TILE // n_acc, n_acc, QW)
        r = a[0]
        for u in range(1, KEY_TILE // n_acc):
            r = op(r, a[u])
        return r

    def idx_product_stage(t, st_ref):
        ks = ki_ref[0, pl.ds(t * KEY_TILE, KEY_TILE), :]
        st_ref[...] = lax.dot_general(ks, q_idx, _NT, preferred_element_type=jnp.float32)

    def idx_combine_stage(t, carry, st_ref, diagonal):
        isc = None
        for h in range(N_IDX_HEADS):
            term = jnp.maximum(st_ref[:, h * QW:(h + 1) * QW], 0.0) * w[h:h + 1, :]
            isc = term if isc is None else isc + term
        lo_src = isc
        if diagonal:
            valid = row <= (qpos - t * KEY_TILE)
            lo_src = jnp.where(valid, isc, jnp.inf)
            isc = jnp.where(valid, isc, -jnp.inf)
        sc_ref[pl.ds(t * KEY_TILE, KEY_TILE), :] = isc
        mn, mx, cge, cgt = carry
        return (jnp.minimum(mn, fold(lo_src, jnp.minimum)),
                jnp.maximum(mx, fold(isc, jnp.maximum)),
                cge + fold((isc >= 0.0).astype(jnp.float32), jnp.add),
                cgt + fold((isc > 0.0).astype(jnp.float32), jnp.add))

    def idx_pair_body(u, carry):
        t = 2 * u
        idx_product_stage(t + 1, stb_ref)
        carry = idx_combine_stage(t, carry, sta_ref, False)
        idx_product_stage(t + 2, sta_ref)
        return idx_combine_stage(t + 1, carry, stb_ref, False)

    n_pairs = (nt - 1) // 2
    t_rem = 2 * n_pairs
    inf_acc = jnp.full((n_acc, QW), jnp.inf, jnp.float32)
    zero_acc = jnp.zeros((n_acc, QW), jnp.float32)
    idx_product_stage(0, sta_ref)
    carry = lax.fori_loop(0, n_pairs, idx_pair_body, (inf_acc, -inf_acc, zero_acc, zero_acc))

    def idx_last_two():
        idx_product_stage(t_rem + 1, stb_ref)
        return idx_combine_stage(t_rem + 1, idx_combine_stage(t_rem, carry, sta_ref, False), stb_ref, True)

    mn, mx, cge, cgt = lax.cond(nt - t_rem == 2, idx_last_two,
                                lambda: idx_combine_stage(t_rem, carry, sta_ref, True))
    smin = jnp.min(mn, axis=0, keepdims=True)
    smax = jnp.max(mx, axis=0, keepdims=True)
    t_first = jnp.min(mx, axis=0, keepdims=True)
    cge0 = jnp.sum(cge, axis=0, keepdims=True)
    cgt0 = jnp.sum(cgt, axis=0, keepdims=True)

    def count_ge(thr):
        def body(t, acc):
            for u in range(KEY_TILE // n_cnt):
                blk = sc_ref[pl.ds(t * KEY_TILE + u * n_cnt, n_cnt), :]
                acc = acc + (blk >= thr).astype(jnp.float32)
            return acc
        acc = lax.fori_loop(0, nt, body, jnp.zeros((n_cnt, QW), jnp.float32))
        return jnp.sum(acc, axis=0, keepdims=True)

    def top_below(bound):
        n_set = 1

        def body(t, accs):
            blk = sc_ref[pl.ds(t * KEY_TILE, KEY_TILE), :].reshape(KEY_TILE // 8, 8, QW)
            accs = [list(a) for a in accs]
            for v in range(KEY_TILE // 8):
                x = jnp.where(blk[v] < bound, blk[v], -jnp.inf)
                lv = accs[v % n_set]
                for i in range(TOP_R):
                    hi_ = jnp.maximum(lv[i], x)
                    if i < TOP_R - 1:
                        x = jnp.minimum(lv[i], x)
                    lv[i] = hi_
            return tuple(tuple(a) for a in accs)

        ninf = jnp.full((8, QW), -jnp.inf, jnp.float32)
        accs = lax.fori_loop(0, nt, body, tuple(tuple(ninf for _ in range(TOP_R)) for _ in range(n_set)))
        cand = jnp.concatenate([a for s_ in accs for a in s_], axis=0)
        ridx = lax.broadcasted_iota(jnp.int32, cand.shape, 0).astype(jnp.float32)
        tops = []
        for _ in range(TOP_R):
            e = jnp.max(cand, axis=0, keepdims=True)
            first = jnp.min(jnp.where(cand == e, ridx, float(cand.shape[0])), axis=0, keepdims=True)
            cand = jnp.where(ridx == first, -jnp.inf, cand)
            tops.append(e)
        return tops

    def all_set(flag):
        return jnp.min(flag) > 0.5

    inf_row = jnp.full((1, QW), jnp.inf, jnp.float32)

    def select_threshold():
        one = jnp.ones((1, QW), jnp.float32)
        zero = jnp.zeros((1, QW), jnp.float32)
        k_aim = kf + AIM_OFFSET

        def logc(c):
            return jnp.log(jnp.maximum(c, 0.5) / k_aim)

        n_valid = (qpos + 1).astype(jnp.float32)
        pos_k = cgt0 >= kf
        neg_k = cge0 < kf
        st = dict(
            lo=jnp.where(pos_k, 0.0, smin), flo=jnp.where(pos_k, logc(cgt0), logc(n_valid)),
            bhi=jnp.where(neg_k, 0.0, smax), fhi=jnp.where(neg_k, logc(cge0), logc(one)),
            hx=jnp.where(neg_k, 0.0, inf_row), chx=jnp.where(neg_k, cge0, zero),
            side=zero,
            found=jnp.where(pos_k | neg_k, zero, one),
            thr=zero, keep=jnp.where(pos_k | neg_k, inf_row, kf - cgt0))

        def probe(st, mid):
            c = count_ge(mid)
            hit = (c == kf) & (st["found"] < 0.5)
            ge = c >= kf
            f = logc(c)
            newside = jnp.where(ge, one, -one)
            same = newside == st["side"]
            fhi = jnp.where(ge & same, st["fhi"] * 0.5, st["fhi"])
            flo = jnp.where(jnp.logical_not(ge) & same, st["flo"] * 0.5, st["flo"])
            return dict(
                lo=jnp.where(ge, mid, st["lo"]), flo=jnp.where(ge, f, flo),
                bhi=jnp.where(ge, st["bhi"], mid), fhi=jnp.where(ge, fhi, f),
                hx=jnp.where(ge, st["hx"], mid), chx=jnp.where(ge, st["chx"], c),
                side=newside,
                found=jnp.where(hit, one, st["found"]),
                thr=jnp.where(hit, mid, st["thr"]), keep=st["keep"])

        def next_mid(st):
            lo, bhi = st["lo"], st["bhi"]
            den = st["flo"] - st["fhi"]
            mid = lo + (bhi - lo) * (st["flo"] / jnp.where(den == 0.0, 1.0, den))
            inside = (mid > lo) & (mid < bhi)
            return jnp.where(inside, mid, lo + (bhi - lo) * 0.5)

        def resolve_top(st):
            tops = top_below(st["hx"])
            r = kf - st["chx"]
            tr = tops[TOP_R - 1]
            for i in range(TOP_R - 2, -1, -1):
                tr = jnp.where(r == float(i + 1), tops[i], tr)
            n_gt = zero
            for i in range(TOP_R - 1):
                n_gt = n_gt + (tops[i] > tr).astype(jnp.float32)
            ok = (st["found"] < 0.5) & (r <= float(TOP_R))
            st = dict(st, found=jnp.where(ok, one, st["found"]), thr=jnp.where(ok, tr, st["thr"]),
                      keep=jnp.where(ok, r - n_gt, st["keep"]))
            return st, tops[0]

        first = jnp.minimum(jnp.maximum(t_first, smin), smax)
        inside = (first > st["lo"]) & (first < st["bhi"])
        st = probe(st, jnp.where(inside, first, next_mid(st)))
        st = dict(st, side=zero)
        st = lax.fori_loop(0, N_INTERP, lambda _, s_: probe(s_, next_mid(s_)), st)
        st, _ = resolve_top(st)

        def fb_body(carry):
            it, st = carry
            st = probe(st, next_mid(st))
            st, u = resolve_top(st)
            descend = it % 2 == 1
            cu = lax.cond(descend, lambda: count_ge(u), lambda: zero)
            ok = (st["found"] < 0.5) & (cu >= kf) & descend
            go = (st["found"] < 0.5) & (cu < kf) & descend
            return it + 1, dict(
                lo=st["lo"], flo=st["flo"],
                bhi=jnp.where(go, jnp.minimum(st["bhi"], u), st["bhi"]),
                fhi=jnp.where(go, logc(cu), st["fhi"]),
                hx=jnp.where(go, u, st["hx"]), chx=jnp.where(go, cu, st["chx"]),
                side=st["side"],
                found=jnp.where(ok, one, st["found"]),
                thr=jnp.where(ok, u, st["thr"]), keep=jnp.where(ok, kf - st["chx"], st["keep"]))

        _, st = lax.while_loop(
            lambda c: jnp.logical_not(all_set(c[1]["found"])) & (c[0] <= 2 * nt * KEY_TILE),
            fb_body, (jnp.int32(0), st))
        return st["thr"], st["keep"]

    thr, keep = lax.cond((j + 1) * QW > top_k, select_threshold,
                         lambda: (jnp.full((1, QW), -F32_MAX, jnp.float32), inf_row))

    NH = N_DSA_HEADS

    def score_stage(t, n_tie, st_ref):
        kd = kd_ref[0, pl.ds(t * KEY_TILE, KEY_TILE), :]
        blk = sc_ref[pl.ds(t * KEY_TILE, KEY_TILE), :]
        tie = blk == thr
        tie_bf = tie.astype(jnp.float32).astype(jnp.bfloat16)
        ranks = []
        for sb in range(KEY_TILE // BLOCK):
            rk = n_tie + jnp.dot(ltri_ref[...], tie_bf[sb * BLOCK:(sb + 1) * BLOCK],
                                 preferred_element_type=jnp.float32)
            n_tie = rk[BLOCK - 1:BLOCK, :]
            ranks.append(rk)
        sel = (blk > thr) | (tie & (jnp.concatenate(ranks, axis=0) <= keep))
        cms = []
        hpd = max(1, 2 * LANES // QW)
        for h0 in range(0, NH, hpd):
            sd = lax.dot_general(kd, q_att[h0 * QW:(h0 + hpd) * QW], _NT,
                                 preferred_element_type=jnp.float32)
            for hh in range(hpd):
                h = h0 + hh
                sdm = jnp.where(sel, sd[:, hh * QW:(hh + 1) * QW], -jnp.inf)
                st_ref[:, h * QW:(h + 1) * QW] = sdm
                cms.append(jnp.max(sdm, axis=0, keepdims=True))
        return jnp.concatenate(cms, axis=1), n_tie

    def value_stage(t, m, acc, cmax, st_ref):
        m_new = jnp.maximum(m, cmax)
        p = jnp.exp2(st_ref[...] - m_new).astype(jnp.bfloat16)
        pv = jnp.dot(vt_ref[0, t], p, preferred_element_type=jnp.float32)
        return m_new, acc * jnp.exp2(m - m_new) + pv

    def pair_body(u, carry):
        m, acc, cmax, n_tie = carry
        t = 2 * u
        cmax_b, n_tie = score_stage(t + 1, n_tie, stb_ref)
        m, acc = value_stage(t, m, acc, cmax, sta_ref)
        cmax, n_tie = score_stage(t + 2, n_tie, sta_ref)
        m, acc = value_stage(t + 1, m, acc, cmax_b, stb_ref)
        return m, acc, cmax, n_tie

    m0 = jnp.full((1, NH * QW), NEG_BIG, jnp.float32)
    a0 = jnp.zeros((VT_ROWS, NH * QW), jnp.float32)
    cmax, n_tie = score_stage(0, jnp.zeros((1, QW), jnp.float32), sta_ref)
    n_pairs = (nt - 1) // 2
    m, acc, cmax, n_tie = lax.fori_loop(0, n_pairs, pair_body, (m0, a0, cmax, n_tie))
    t_rem = 2 * n_pairs

    def last_two():
        cmax_l, _ = score_stage(t_rem + 1, n_tie, stb_ref)
        m_, acc_ = value_stage(t_rem, m, acc, cmax, sta_ref)
        return value_stage(t_rem + 1, m_, acc_, cmax_l, stb_ref)[1]

    acc = lax.cond(nt - t_rem == 2, last_two, lambda: value_stage(t_rem, m, acc, cmax, sta_ref)[1])
    ot = acc[:HEAD_DIM] / acc[HEAD_DIM:HEAD_DIM + 1]
    for p in range(NH // 2):
        pair = jnp.concatenate([ot[:, (2 * p) * QW:(2 * p + 1) * QW],
                                ot[:, (2 * p + 1) * QW:(2 * p + 2) * QW]], axis=0)
        o_ref[0, :, p * LANES:(p + 1) * LANES] = pair.T.astype(jnp.bfloat16)


def _dsa(qi, wit, qd, ki, kd, vdt, top_k):
    B, S, _ = qi.shape
    assert S % KEY_TILE == 0 and KEY_TILE % BLOCK == 0 and S % DSA_Q == 0 and DSA_Q % LANES == 0
    assert top_k % DSA_Q == 0, "a grid step must not mix queries with <= top_k and > top_k valid keys"
    blk = lambda n: pl.BlockSpec((1, DSA_Q, n), lambda b, j: (b, j, 0))
    whole = lambda n: pl.BlockSpec((1, S, n), lambda b, j: (b, 0, 0))
    ltri = jnp.tril(jnp.ones((BLOCK, BLOCK), jnp.bfloat16))
    return pl.pallas_call(
        functools.partial(_dsa_kernel, top_k=top_k),
        grid=(B, S // DSA_Q),
        in_specs=[blk(2 * LANES),
                  pl.BlockSpec((1, 8, DSA_Q), lambda b, j: (b, 0, j)),
                  blk(2 * LANES), whole(LANES), whole(LANES),
                  pl.BlockSpec((1, S // KEY_TILE, VT_ROWS, KEY_TILE), lambda b, j: (b, 0, 0, 0)),
                  _const_spec(ltri.shape)],
        out_specs=blk(DSA_WIDTH),
        out_shape=jax.ShapeDtypeStruct((B, S, DSA_WIDTH), jnp.bfloat16),
        scratch_shapes=[pltpu.VMEM((S, DSA_Q), jnp.float32),
                        pltpu.VMEM((KEY_TILE, N_DSA_HEADS * DSA_Q), jnp.float32),
                        pltpu.VMEM((KEY_TILE, N_DSA_HEADS * DSA_Q), jnp.float32)],
        compiler_params=_cparams(2),
        name="dsa",
    )(qi, wit, qd, ki, kd, vdt, ltri)


def _merge_kernel(x_ref, gates_ref, os_ref, od_ref, om_ref, wps_ref, wpd_ref, wpm_ref, wo_ref,
                  gm_ref, w1_ref, w2_ref, gf_ref, out_ref):
    f32 = jnp.float32
    x = x_ref[...]
    merged = None
    for b, (o_ref, w_ref) in enumerate(((os_ref, wps_ref), (od_ref, wpd_ref), (om_ref, wpm_ref))):
        y = jnp.dot(o_ref[...], w_ref[...], preferred_element_type=f32)
        y = y * gates_ref[:, b * D_MODEL:(b + 1) * D_MODEL].astype(f32)
        merged = y if merged is None else merged + y
    x1 = x + jnp.dot(merged.astype(jnp.bfloat16), wo_ref[...], preferred_element_type=f32)
    h = (x1 * _rms_scale(x1) * gm_ref[...]).astype(jnp.bfloat16)
    acc = x1
    FC = 1024
    for c in range(D_FF // FC):
        hid = jnp.maximum(jnp.dot(h, w1_ref[:, c * FC:(c + 1) * FC], preferred_element_type=f32), 0.0)
        hid = (hid * hid).astype(jnp.bfloat16)
        acc = acc + jnp.dot(hid, w2_ref[c * FC:(c + 1) * FC, :], preferred_element_type=f32)
    out_ref[...] = acc * _rms_scale(acc) * gf_ref[...]


def _merge(x2d, gates, o_swa, o_dsa, o_mem, w_proj_swa, w_proj_dsa, w_proj_mem, w_out,
           g_mlp, w_mlp_in, w_mlp_out, g_final):
    N = x2d.shape[0]
    T = TOK_TILE
    bf = jnp.bfloat16
    tok = lambda n: pl.BlockSpec((T, n), lambda i: (i, 0))
    ws = [w_proj_swa.astype(bf), w_proj_dsa.astype(bf), w_proj_mem.astype(bf), w_out.astype(bf)]
    w1, w2 = w_mlp_in.astype(bf), w_mlp_out.astype(bf)
    return pl.pallas_call(
        _merge_kernel,
        grid=(N // T,),
        in_specs=[tok(D_MODEL), tok(N_BRANCH * D_MODEL), tok(SWA_WIDTH), tok(DSA_WIDTH), tok(MEM_WIDTH)]
                 + [_const_spec(a.shape) for a in ws]
                 + [_const_spec((1, D_MODEL)), _const_spec(w1.shape), _const_spec(w2.shape),
                    _const_spec((1, D_MODEL))],
        out_specs=tok(D_MODEL),
        out_shape=jax.ShapeDtypeStruct((N, D_MODEL), jnp.float32),
        compiler_params=_cparams(1),
        name="merge",
    )(x2d, gates, o_swa, o_dsa, o_mem, *ws, g_mlp.reshape(1, D_MODEL), w1, w2,
      g_final.reshape(1, D_MODEL))


def kernel(x, mem, positions, g_mix, w_in, b_gate, sinks, g_mem, w_mem_kv, w_proj_swa, w_proj_dsa,
           w_proj_mem, w_out, g_mlp, w_mlp_in, w_mlp_out, g_final):
    B, S, D = x.shape
    assert g_mix.shape[0] == 1, "the final norm is fused into the single layer's merge kernel"
    top_k = min(TOPK_MAX, S // 4)
    for l in range(1):
        mk, mv = _memkv(mem, g_mem[l], w_mem_kv[l])
        (qs, ks, vs, qd, kd, qi, ki, vdt, wit, gates, o_mem) = _proj(
            x, positions, g_mix[l], w_in[l], b_gate[l], mk, mv)
        o_swa = _swa(qs, ks, vs, sinks[l])
        o_dsa = _dsa(qi, wit, qd, ki, kd, vdt, top_k)
        x = _merge(x.reshape(B * S, D), gates.reshape(B * S, -1), o_swa.reshape(B * S, -1),
                   o_dsa.reshape(B * S, -1), o_mem.reshape(B * S, -1), w_proj_swa[l], w_proj_dsa[l],
                   w_proj_mem[l], w_out[l], g_mlp[l], w_mlp_in[l], w_mlp_out[l], g_final).reshape(B, S, D)
    return x
```

```python
import functools

import jax
import jax.numpy as jnp
import numpy as np
from jax import lax
from jax.experimental import pallas as pl
from jax.experimental.pallas import tpu as pltpu

D_MODEL = 1024
HEAD_DIM = 64
N_SWA_HEADS = 8
N_SWA_KV = 2
WINDOW = 128
BLOCK = 128
N_DSA_HEADS = 4
N_IDX_HEADS = 4
IDX_DIM = 64
TOPK_MAX = 256
N_MEM_HEADS = 4
N_BRANCH = 3
D_FF = 4 * D_MODEL
ROPE_THETA = 10000.0
EPS = 1e-6

SWA_WIDTH = N_SWA_HEADS * HEAD_DIM
SWA_KV_WIDTH = N_SWA_KV * HEAD_DIM
DSA_WIDTH = N_DSA_HEADS * HEAD_DIM
MEM_WIDTH = N_MEM_HEADS * HEAD_DIM
IN_SPLITS = (SWA_WIDTH, SWA_KV_WIDTH, SWA_KV_WIDTH,
             DSA_WIDTH, HEAD_DIM, HEAD_DIM,
             N_IDX_HEADS * IDX_DIM, IDX_DIM, N_IDX_HEADS,
             MEM_WIDTH, N_BRANCH * D_MODEL)

LANES = 128
TOK_TILE = 512
PROJ_TILE = 512
KEY_TILE = 512
DSA_Q = 256
VMEM_LIMIT = 56 * 1024 * 1024
NEG_BIG = -1e30
F32_MAX = float(np.finfo(np.float32).max)
N_INTERP = 6
TOP_R = 4
AIM_OFFSET = -2.5

ATT_SCALE = HEAD_DIM ** -0.5
IDX_SCALE = (N_IDX_HEADS ** -0.5) * (IDX_DIM ** -0.5)

G_QS, G_KS, G_QD, G_KD, G_QI, G_KI = 0, 4, 6, 8, 9, 11
N_ROPE_GROUPS = 12
G_VS, G_QM = 12, 14
N_RM_GROUPS = 16

WT_ROWS = 80
VT_ROWS = 80
LOG2E = float(np.log2(np.e))

_NT = (((1,), (1,)), ((), ()))


def _cparams(n_grid):
    return pltpu.CompilerParams(
        dimension_semantics=("arbitrary",) * n_grid, vmem_limit_bytes=VMEM_LIMIT)


def _const_spec(shape):
    nd = len(shape)
    return pl.BlockSpec(shape, lambda *_: (0,) * nd, pipeline_mode=pl.Buffered(1))


def _rms_scale(x):
    return lax.rsqrt(jnp.mean(x * x, axis=-1, keepdims=True) + EPS)


def _first_half_mask():
    lane = lax.broadcasted_iota(jnp.int32, (1, LANES), 1)
    return (lane % HEAD_DIM) < (HEAD_DIM // 2), lane < HEAD_DIM


def _stack_heads(q, n_heads, low64):
    parts = []
    for h in range(n_heads):
        col = q[:, (h // 2) * LANES:(h // 2 + 1) * LANES]
        keep = low64 if h % 2 == 0 else jnp.logical_not(low64)
        parts.append(jnp.where(keep, col, jnp.zeros_like(col)))
    return jnp.concatenate(parts, axis=0)


def _memkv_kernel(mem_ref, g_ref, w_ref, k_ref, v_ref):
    m = mem_ref[0]
    mn = (m * _rms_scale(m) * g_ref[...]).astype(jnp.bfloat16)
    kv = jnp.dot(mn, w_ref[...], preferred_element_type=jnp.float32)
    k_ref[0] = kv[:, :MEM_WIDTH].astype(jnp.bfloat16)
    v_ref[0] = kv[:, MEM_WIDTH:].astype(jnp.bfloat16)


def _memkv(mem, g_mem, w_mem_kv):
    B, M, _ = mem.shape
    return pl.pallas_call(
        _memkv_kernel,
        grid=(B,),
        in_specs=[pl.BlockSpec((1, M, D_MODEL), lambda b: (b, 0, 0)),
                  _const_spec((1, D_MODEL)),
                  _const_spec((D_MODEL, 2 * MEM_WIDTH))],
        out_specs=[pl.BlockSpec((1, M, MEM_WIDTH), lambda b: (b, 0, 0)),
                   pl.BlockSpec((1, M, MEM_WIDTH), lambda b: (b, 0, 0))],
        out_shape=[jax.ShapeDtypeStruct((B, M, MEM_WIDTH), jnp.bfloat16)] * 2,
        compiler_params=_cparams(1),
        name="memkv",
    )(mem, g_mem.reshape(1, D_MODEL), w_mem_kv.astype(jnp.bfloat16))


def _proj_kernel(sink_ref, x_ref, pos_ref, g_ref, invf_ref, wrm_ref, wt_ref, wg_ref, bg_ref, mk_ref, mv_ref,
                 qd_ref, kd_ref, qi_ref, ki_ref, vdt_ref, wit_ref, gates_ref, omem_ref, oswa_ref,
                 qs_ref, ks_ref, vs_ref):
    tile = pl.program_id(1)

    t_rows = x_ref.shape[1]

    @pl.when(tile == 0)
    def _():
        ks_ref[:BLOCK, :] = jnp.zeros((BLOCK, ks_ref.shape[1]), ks_ref.dtype)
        vs_ref[:BLOCK, :] = jnp.zeros((BLOCK, vs_ref.shape[1]), vs_ref.dtype)

    @pl.when(tile > 0)
    def _():
        ks_ref[:BLOCK, :] = ks_ref[t_rows:t_rows + BLOCK, :]
        vs_ref[:BLOCK, :] = vs_ref[t_rows:t_rows + BLOCK, :]

    x = x_ref[0]
    h = (x * _rms_scale(x) * g_ref[...]).astype(jnp.bfloat16)
    half32, low64 = _first_half_mask()

    ang = pos_ref[0].astype(jnp.float32) * invf_ref[...]
    cos = jnp.cos(ang)
    sin = jnp.where(half32, -jnp.sin(ang), jnp.sin(ang))

    def rope(y):
        partner = jnp.where(half32, pltpu.roll(y, LANES - HEAD_DIM // 2, 1),
                            pltpu.roll(y, HEAD_DIM // 2, 1))
        return y * cos + partner * sin

    T = x.shape[0]
    dest = {}
    for g0, (ref, n, lead, rows) in {
            G_QS: (qs_ref, 4, (), slice(0, T)), G_KS: (ks_ref, 2, (), slice(BLOCK, BLOCK + T)),
            G_VS: (vs_ref, 2, (), slice(BLOCK, BLOCK + T)),
            G_QD: (qd_ref, 2, (0,), slice(0, T)), G_KD: (kd_ref, 1, (0,), slice(0, T)),
            G_QI: (qi_ref, 2, (0,), slice(0, T)), G_KI: (ki_ref, 1, (0,), slice(0, T))}.items():
        for i in range(n):
            dest[g0 + i] = (ref, lead, rows, i)
    def group_pair(g):
        y2 = jnp.dot(h, wrm_ref[:, g * LANES:(g + 2) * LANES], preferred_element_type=jnp.float32)
        return [y2[:, u * LANES:(u + 1) * LANES] for u in range(2)]

    def rope_pair(g):
        for u, y in enumerate(group_pair(g)):
            if g + u < N_ROPE_GROUPS:
                y = rope(y)
            ref, lead, rows, i = dest[g + u]
            ref[lead + (rows, slice(i * LANES, (i + 1) * LANES))] = y.astype(jnp.bfloat16)

    def gate_chunk(c):
        z = jnp.dot(h, wg_ref[:, c * 512:(c + 1) * 512], preferred_element_type=jnp.float32)
        z = z + bg_ref[:, c * 512:(c + 1) * 512]
        gates_ref[0, :, c * 512:(c + 1) * 512] = jax.nn.sigmoid(z).astype(jnp.bfloat16)

    mem_parts = []
    for p, qm in enumerate(group_pair(G_QM)):
        qm = qm.astype(jnp.bfloat16)
        mk = mk_ref[0, :, p * LANES:(p + 1) * LANES]
        for hh in range(2):
            keep = low64 if hh == 0 else jnp.logical_not(low64)
            qh = jnp.where(keep, qm, jnp.zeros_like(qm))
            s = lax.dot_general(qh, mk, _NT, preferred_element_type=jnp.float32)
            e = jnp.exp2(s - jnp.max(s, axis=-1, keepdims=True))
            mem_parts.append((e.astype(jnp.bfloat16), jnp.sum(e, axis=-1, keepdims=True)))

    n_gc = (N_BRANCH * D_MODEL) // 512
    for c in range(max(n_gc, G_QM // 2)):
        if c < G_QM // 2:
            rope_pair(2 * c)
        if c < n_gc:
            gate_chunk(c)

    _swa_blocks(tile, sink_ref, qs_ref[...], ks_ref[...], vs_ref[...], oswa_ref, range(T // BLOCK))

    yt = lax.dot_general(wt_ref[...], h, _NT, preferred_element_type=jnp.float32)
    pad_row = lax.broadcasted_iota(jnp.int32, (VT_ROWS - HEAD_DIM, KEY_TILE), 0)
    ones_row = jnp.where(pad_row == 0, 1.0, 0.0).astype(jnp.bfloat16)
    for u in range(vdt_ref.shape[1]):
        vdt_ref[0, u, :HEAD_DIM, :] = yt[:HEAD_DIM, u * KEY_TILE:(u + 1) * KEY_TILE].astype(jnp.bfloat16)
        vdt_ref[0, u, HEAD_DIM:, :] = ones_row
    wit_ref[0] = yt[HEAD_DIM:HEAD_DIM + 8]

    for p in range(N_MEM_HEADS // 2):
        mv = mv_ref[0, :, p * LANES:(p + 1) * LANES]
        o_pair = None
        for hh in range(2):
            e, l = mem_parts[2 * p + hh]
            o = jnp.dot(e, mv, preferred_element_type=jnp.float32) / l
            o_pair = o if hh == 0 else jnp.where(low64, o_pair, o)
        omem_ref[0, :, p * LANES:(p + 1) * LANES] = o_pair.astype(jnp.bfloat16)


def _proj(x, positions, g_mix, w_in, b_gate, mk, mv, sinks):
    B, S, _ = x.shape
    M = mk.shape[1]
    T = PROJ_TILE
    assert S % T == 0 and T % KEY_TILE == 0
    sp = np.cumsum((0,) + IN_SPLITS)
    w = {n: w_in[:, sp[i]:sp[i + 1]] for i, n in enumerate(
        ("qs", "ks", "vs", "qd", "kd", "vd", "qi", "ki", "wi", "qm", "gate"))}

    def dup(a):
        return jnp.concatenate([a, a], axis=1)

    wrm = jnp.concatenate(
        [w["qs"] * (ATT_SCALE * LOG2E),
         dup(w["ks"][:, :HEAD_DIM]), dup(w["ks"][:, HEAD_DIM:]),
         w["qd"] * (ATT_SCALE * LOG2E), dup(w["kd"]),
         w["qi"], dup(w["ki"]),
         dup(w["vs"][:, :HEAD_DIM]), dup(w["vs"][:, HEAD_DIM:]),
         w["qm"] * (ATT_SCALE * LOG2E)], axis=1).astype(jnp.bfloat16)
    assert wrm.shape[1] == N_RM_GROUPS * LANES
    wt = jnp.concatenate(
        [w["vd"].T, w["wi"].T * IDX_SCALE, jnp.zeros((WT_ROWS - HEAD_DIM - N_IDX_HEADS, D_MODEL), w_in.dtype)],
        axis=0).astype(jnp.bfloat16)
    wg = w["gate"].astype(jnp.bfloat16)
    half = HEAD_DIM // 2
    invf = jnp.power(ROPE_THETA, -jnp.arange(half, dtype=jnp.float32) / half)
    invf = jnp.tile(invf, LANES // half).reshape(1, LANES)

    tok = lambda n: pl.BlockSpec((1, T, n), lambda b, i: (b, i, 0))
    bf = jnp.bfloat16
    out_shape = [
        jax.ShapeDtypeStruct((B, S, 2 * LANES), bf),
        jax.ShapeDtypeStruct((B, S, LANES), bf),
        jax.ShapeDtypeStruct((B, S, 2 * LANES), bf),
        jax.ShapeDtypeStruct((B, S, LANES), bf),
        jax.ShapeDtypeStruct((B, S // KEY_TILE, VT_ROWS, KEY_TILE), bf),
        jax.ShapeDtypeStruct((B, 8, S), jnp.float32),
        jax.ShapeDtypeStruct((B, S, N_BRANCH * D_MODEL), bf),
        jax.ShapeDtypeStruct((B, S, MEM_WIDTH), bf),
        jax.ShapeDtypeStruct((B, S, SWA_WIDTH), bf),
    ]
    out_specs = [tok(2 * LANES), tok(LANES), tok(2 * LANES), tok(LANES),
                 pl.BlockSpec((1, T // KEY_TILE, VT_ROWS, KEY_TILE), lambda b, i: (b, i, 0, 0)),
                 pl.BlockSpec((1, 8, T), lambda b, i: (b, 0, i)),
                 tok(N_BRANCH * D_MODEL), tok(MEM_WIDTH), tok(SWA_WIDTH)]
    return pl.pallas_call(
        _proj_kernel,
        grid=(B, S // T),
        in_specs=[pl.BlockSpec(memory_space=pltpu.SMEM),
                  tok(D_MODEL),
                  pl.BlockSpec((1, T, 1), lambda b, i: (b, i, 0)),
                  _const_spec((1, D_MODEL)), _const_spec((1, LANES)),
                  _const_spec(wrm.shape), _const_spec(wt.shape), _const_spec(wg.shape),
                  _const_spec((1, N_BRANCH * D_MODEL)),
                  pl.BlockSpec((1, M, MEM_WIDTH), lambda b, i: (b, 0, 0)),
                  pl.BlockSpec((1, M, MEM_WIDTH), lambda b, i: (b, 0, 0))],
        out_specs=out_specs,
        out_shape=out_shape,
        scratch_shapes=[pltpu.VMEM((T, 4 * LANES), bf), pltpu.VMEM((BLOCK + T, 2 * LANES), bf),
                        pltpu.VMEM((BLOCK + T, 2 * LANES), bf)],
        compiler_params=_cparams(2),
        name="proj",
    )(sinks, x, positions.reshape(B, S, 1), g_mix.reshape(1, D_MODEL), invf, wrm, wt, wg,
      b_gate.reshape(1, -1), mk, mv)


def _swa_blocks(i, sink_ref, q, kcat, vcat, o_ref, blocks):
    _, low64 = _first_half_mask()
    qi = lax.broadcasted_iota(jnp.int32, (BLOCK, 2 * BLOCK), 0)
    sj = lax.broadcasted_iota(jnp.int32, (BLOCK, 2 * BLOCK), 1)
    dist = qi + BLOCK - sj
    band = (dist >= 0) & (dist < WINDOW)
    G = N_SWA_HEADS // N_SWA_KV
    for bq in blocks:
        valid = band
        if bq == 0:
            valid = band & ((sj + jnp.where(i > 0, BLOCK, 0)) >= BLOCK)
        qb = q[bq * BLOCK:(bq + 1) * BLOCK]
        for g in range(N_SWA_KV):
            qst = _stack_heads(qb[:, g * G * HEAD_DIM:(g + 1) * G * HEAD_DIM], G, low64)
            kw = kcat[bq * BLOCK:(bq + 2) * BLOCK, g * LANES:(g + 1) * LANES]
            vw = vcat[bq * BLOCK:(bq + 2) * BLOCK, g * LANES:(g + 1) * LANES]
            s = lax.dot_general(qst, kw, _NT, preferred_element_type=jnp.float32)
            es, ls = [], []
            for hh in range(G):
                sh = jnp.where(valid, s[hh * BLOCK:(hh + 1) * BLOCK], -jnp.inf)
                sink = sink_ref[g * G + hh] * LOG2E
                m = jnp.maximum(jnp.max(sh, axis=-1, keepdims=True), sink)
                e = jnp.exp2(sh - m)
                ls.append(jnp.sum(e, axis=-1, keepdims=True) + jnp.exp2(sink - m))
                es.append(e.astype(jnp.bfloat16))
            o = jnp.dot(jnp.concatenate(es, axis=0), vw, preferred_element_type=jnp.float32)
            for pp in range(G // 2):
                o0 = o[(2 * pp) * BLOCK:(2 * pp + 1) * BLOCK] / ls[2 * pp]
                o1 = o[(2 * pp + 1) * BLOCK:(2 * pp + 2) * BLOCK] / ls[2 * pp + 1]
                c0 = (g * (G // 2) + pp) * LANES
                o_ref[0, bq * BLOCK:(bq + 1) * BLOCK, c0:c0 + LANES] = (
                    jnp.where(low64, o0, o1).astype(jnp.bfloat16))


def _dsa_kernel(qi_ref, wi_ref, qd_ref, ki_ref, kd_ref, vt_ref, ltri_ref, o_ref, sc_ref, sta_ref, stb_ref,
                *, top_k):
    j = pl.program_id(1)
    QW = DSA_Q
    nt = ((j + 1) * QW + KEY_TILE - 1) // KEY_TILE
    _, low64 = _first_half_mask()
    lane = lax.broadcasted_iota(jnp.int32, (1, QW), 1)
    qpos = j * QW + lane
    w = wi_ref[0]
    q_idx = _stack_heads(qi_ref[0], N_IDX_HEADS, low64)
    q_att = _stack_heads(qd_ref[0], N_DSA_HEADS, low64)
    kf = jnp.float32(top_k)
    row = lax.broadcasted_iota(jnp.int32, (KEY_TILE, QW), 0)
    n_acc = 64
    n_cnt = 32

    def fold(a, op):
        a = a.reshape(KEY_TILE // n_acc, n_acc, QW)
        r = a[0]
        for u in range(1, KEY_TILE // n_acc):
            r = op(r, a[u])
        return r

    def idx_product_stage(t, st_ref):
        ks = ki_ref[0, pl.ds(t * KEY_TILE, KEY_TILE), :]
        st_ref[...] = lax.dot_general(ks, q_idx, _NT, preferred_element_type=jnp.float32)

    def idx_combine_stage(t, carry, st_ref, diagonal):
        isc = None
        for h in range(N_IDX_HEADS):
            term = jnp.maximum(st_ref[:, h * QW:(h + 1) * QW], 0.0) * w[h:h + 1, :]
            isc = term if isc is None else isc + term
        lo_src = isc
        if diagonal:
            valid = row <= (qpos - t * KEY_TILE)
            lo_src = jnp.where(valid, isc, jnp.inf)
            isc = jnp.where(valid, isc, -jnp.inf)
        sc_ref[pl.ds(t * KEY_TILE, KEY_TILE), :] = isc
        mn, mx, cge, cgt = carry
        return (jnp.minimum(mn, fold(lo_src, jnp.minimum)),
                jnp.maximum(mx, fold(isc, jnp.maximum)),
                cge + fold((isc >= 0.0).astype(jnp.float32), jnp.add),
                cgt + fold((isc > 0.0).astype(jnp.float32), jnp.add))

    def idx_pair_body(u, carry):
        t = 2 * u
        idx_product_stage(t + 1, stb_ref)
        carry = idx_combine_stage(t, carry, sta_ref, False)
        idx_product_stage(t + 2, sta_ref)
        return idx_combine_stage(t + 1, carry, stb_ref, False)

    n_pairs = (nt - 1) // 2
    t_rem = 2 * n_pairs
    inf_acc = jnp.full((n_acc, QW), jnp.inf, jnp.float32)
    zero_acc = jnp.zeros((n_acc, QW), jnp.float32)
    idx_product_stage(0, sta_ref)
    carry = lax.fori_loop(0, n_pairs, idx_pair_body, (inf_acc, -inf_acc, zero_acc, zero_acc))

    def idx_last_two():
        idx_product_stage(t_rem + 1, stb_ref)
        return idx_combine_stage(t_rem + 1, idx_combine_stage(t_rem, carry, sta_ref, False), stb_ref, True)

    mn, mx, cge, cgt = lax.cond(nt - t_rem == 2, idx_last_two,
                                lambda: idx_combine_stage(t_rem, carry, sta_ref, True))
    smin = jnp.min(mn, axis=0, keepdims=True)
    smax = jnp.max(mx, axis=0, keepdims=True)
    t_first = jnp.min(mx, axis=0, keepdims=True)
    cge0 = jnp.sum(cge, axis=0, keepdims=True)
    cgt0 = jnp.sum(cgt, axis=0, keepdims=True)

    def count_ge(thr):
        def body(t, acc):
            for u in range(KEY_TILE // n_cnt):
                blk = sc_ref[pl.ds(t * KEY_TILE + u * n_cnt, n_cnt), :]
                acc = acc + (blk >= thr).astype(jnp.float32)
            return acc
        acc = lax.fori_loop(0, nt, body, jnp.zeros((n_cnt, QW), jnp.float32))
        return jnp.sum(acc, axis=0, keepdims=True)

    def top_below(bound):
        n_set = 1

        def body(t, accs):
            blk = sc_ref[pl.ds(t * KEY_TILE, KEY_TILE), :].reshape(KEY_TILE // 8, 8, QW)
            accs = [list(a) for a in accs]
            for v in range(KEY_TILE // 8):
                x = jnp.where(blk[v] < bound, blk[v], -jnp.inf)
                lv = accs[v % n_set]
                for i in range(TOP_R):
                    hi_ = jnp.maximum(lv[i], x)
                    if i < TOP_R - 1:
                        x = jnp.minimum(lv[i], x)
                    lv[i] = hi_
            return tuple(tuple(a) for a in accs)

        ninf = jnp.full((8, QW), -jnp.inf, jnp.float32)
        accs = lax.fori_loop(0, nt, body, tuple(tuple(ninf for _ in range(TOP_R)) for _ in range(n_set)))
        cand = jnp.concatenate([a for s_ in accs for a in s_], axis=0)
        ridx = lax.broadcasted_iota(jnp.int32, cand.shape, 0).astype(jnp.float32)
        tops = []
        for _ in range(TOP_R):
            e = jnp.max(cand, axis=0, keepdims=True)
            first = jnp.min(jnp.where(cand == e, ridx, float(cand.shape[0])), axis=0, keepdims=True)
            cand = jnp.where(ridx == first, -jnp.inf, cand)
            tops.append(e)
        return tops

    def all_set(flag):
        return jnp.min(flag) > 0.5

    inf_row = jnp.full((1, QW), jnp.inf, jnp.float32)

    def select_threshold():
        one = jnp.ones((1, QW), jnp.float32)
        zero = jnp.zeros((1, QW), jnp.float32)
        k_aim = kf + AIM_OFFSET

        def logc(c):
            return jnp.log(jnp.maximum(c, 0.5) / k_aim)

        n_valid = (qpos + 1).astype(jnp.float32)
        pos_k = cgt0 >= kf
        neg_k = cge0 < kf
        st = dict(
            lo=jnp.where(pos_k, 0.0, smin), flo=jnp.where(pos_k, logc(cgt0), logc(n_valid)),
            bhi=jnp.where(neg_k, 0.0, smax), fhi=jnp.where(neg_k, logc(cge0), logc(one)),
            hx=jnp.where(neg_k, 0.0, inf_row), chx=jnp.where(neg_k, cge0, zero),
            side=zero,
            found=jnp.where(pos_k | neg_k, zero, one),
            thr=zero, keep=jnp.where(pos_k | neg_k, inf_row, kf - cgt0))

        def probe(st, mid):
            c = count_ge(mid)
            hit = (c == kf) & (st["found"] < 0.5)
            ge = c >= kf
            f = logc(c)
            newside = jnp.where(ge, one, -one)
            same = newside == st["side"]
            fhi = jnp.where(ge & same, st["fhi"] * 0.5, st["fhi"])
            flo = jnp.where(jnp.logical_not(ge) & same, st["flo"] * 0.5, st["flo"])
            return dict(
                lo=jnp.where(ge, mid, st["lo"]), flo=jnp.where(ge, f, flo),
                bhi=jnp.where(ge, st["bhi"], mid), fhi=jnp.where(ge, fhi, f),
                hx=jnp.where(ge, st["hx"], mid), chx=jnp.where(ge, st["chx"], c),
                side=newside,
                found=jnp.where(hit, one, st["found"]),
                thr=jnp.where(hit, mid, st["thr"]), keep=st["keep"])

        def next_mid(st):
            lo, bhi = st["lo"], st["bhi"]
            den = st["flo"] - st["fhi"]
            mid = lo + (bhi - lo) * (st["flo"] / jnp.where(den == 0.0, 1.0, den))
            inside = (mid > lo) & (mid < bhi)
            return jnp.where(inside, mid, lo + (bhi - lo) * 0.5)

        def resolve_top(st):
            tops = top_below(st["hx"])
            r = kf - st["chx"]
            tr = tops[TOP_R - 1]
            for i in range(TOP_R - 2, -1, -1):
                tr = jnp.where(r == float(i + 1), tops[i], tr)
            n_gt = zero
            for i in range(TOP_R - 1):
                n_gt = n_gt + (tops[i] > tr).astype(jnp.float32)
            ok = (st["found"] < 0.5) & (r <= float(TOP_R))
            st = dict(st, found=jnp.where(ok, one, st["found"]), thr=jnp.where(ok, tr, st["thr"]),
                      keep=jnp.where(ok, r - n_gt, st["keep"]))
            return st, tops[0]

        first = jnp.minimum(jnp.maximum(t_first, smin), smax)
        inside = (first > st["lo"]) & (first < st["bhi"])
        st = probe(st, jnp.where(inside, first, next_mid(st)))
        st = dict(st, side=zero)
        st = lax.fori_loop(0, N_INTERP, lambda _, s_: probe(s_, next_mid(s_)), st)
        st, _ = resolve_top(st)

        def fb_body(carry):
            it, st = carry
            st = probe(st, next_mid(st))
            st, u = resolve_top(st)
            descend = it % 2 == 1
            cu = lax.cond(descend, lambda: count_ge(u), lambda: zero)
            ok = (st["found"] < 0.5) & (cu >= kf) & descend
            go = (st["found"] < 0.5) & (cu < kf) & descend
            return it + 1, dict(
                lo=st["lo"], flo=st["flo"],
                bhi=jnp.where(go, jnp.minimum(st["bhi"], u), st["bhi"]),
                fhi=jnp.where(go, logc(cu), st["fhi"]),
                hx=jnp.where(go, u, st["hx"]), chx=jnp.where(go, cu, st["chx"]),
                side=st["side"],
                found=jnp.where(ok, one, st["found"]),
                thr=jnp.where(ok, u, st["thr"]), keep=jnp.where(ok, kf - st["chx"], st["keep"]))

        _, st = lax.while_loop(
            lambda c: jnp.logical_not(all_set(c[1]["found"])) & (c[0] <= 2 * nt * KEY_TILE),
            fb_body, (jnp.int32(0), st))
        return st["thr"], st["keep"]

    thr, keep = lax.cond((j + 1) * QW > top_k, select_threshold,
                         lambda: (jnp.full((1, QW), -F32_MAX, jnp.float32), inf_row))

    NH = N_DSA_HEADS

    def score_stage(t, n_tie, st_ref):
        kd = kd_ref[0, pl.ds(t * KEY_TILE, KEY_TILE), :]
        blk = sc_ref[pl.ds(t * KEY_TILE, KEY_TILE), :]
        tie = blk == thr
        tie_bf = tie.astype(jnp.float32).astype(jnp.bfloat16)
        ranks = []
        for sb in range(KEY_TILE // BLOCK):
            rk = n_tie + jnp.dot(ltri_ref[...], tie_bf[sb * BLOCK:(sb + 1) * BLOCK],
                                 preferred_element_type=jnp.float32)
            n_tie = rk[BLOCK - 1:BLOCK, :]
            ranks.append(rk)
        sel = (blk > thr) | (tie & (jnp.concatenate(ranks, axis=0) <= keep))
        cms = []
        hpd = max(1, 2 * LANES // QW)
        for h0 in range(0, NH, hpd):
            sd = lax.dot_general(kd, q_att[h0 * QW:(h0 + hpd) * QW], _NT,
                                 preferred_element_type=jnp.float32)
            for hh in range(hpd):
                h = h0 + hh
                sdm = jnp.where(sel, sd[:, hh * QW:(hh + 1) * QW], -jnp.inf)
                st_ref[:, h * QW:(h + 1) * QW] = sdm
                cms.append(jnp.max(sdm, axis=0, keepdims=True))
        return jnp.concatenate(cms, axis=1), n_tie

    def value_stage(t, m, acc, cmax, st_ref):
        m_new = jnp.maximum(m, cmax)
        p = jnp.exp2(st_ref[...] - m_new).astype(jnp.bfloat16)
        pv = jnp.dot(vt_ref[0, t], p, preferred_element_type=jnp.float32)
        return m_new, acc * jnp.exp2(m - m_new) + pv

    def pair_body(u, carry):
        m, acc, cmax, n_tie = carry
        t = 2 * u
        cmax_b, n_tie = score_stage(t + 1, n_tie, stb_ref)
        m, acc = value_stage(t, m, acc, cmax, sta_ref)
        cmax, n_tie = score_stage(t + 2, n_tie, sta_ref)
        m, acc = value_stage(t + 1, m, acc, cmax_b, stb_ref)
        return m, acc, cmax, n_tie

    m0 = jnp.full((1, NH * QW), NEG_BIG, jnp.float32)
    a0 = jnp.zeros((VT_ROWS, NH * QW), jnp.float32)
    cmax, n_tie = score_stage(0, jnp.zeros((1, QW), jnp.float32), sta_ref)
    n_pairs = (nt - 1) // 2
    m, acc, cmax, n_tie = lax.fori_loop(0, n_pairs, pair_body, (m0, a0, cmax, n_tie))
    t_rem = 2 * n_pairs

    def last_two():
        cmax_l, _ = score_stage(t_rem + 1, n_tie, stb_ref)
        m_, acc_ = value_stage(t_rem, m, acc, cmax, sta_ref)
        return value_stage(t_rem + 1, m_, acc_, cmax_l, stb_ref)[1]

    acc = lax.cond(nt - t_rem == 2, last_two, lambda: value_stage(t_rem, m, acc, cmax, sta_ref)[1])
    ot = acc[:HEAD_DIM] / acc[HEAD_DIM:HEAD_DIM + 1]
    for p in range(NH // 2):
        pair = jnp.concatenate([ot[:, (2 * p) * QW:(2 * p + 1) * QW],
                                ot[:, (2 * p + 1) * QW:(2 * p + 2) * QW]], axis=0)
        o_ref[0, :, p * LANES:(p + 1) * LANES] = pair.T.astype(jnp.bfloat16)


def _dsa(qi, wit, qd, ki, kd, vdt, top_k):
    B, S, _ = qi.shape
    assert S % KEY_TILE == 0 and KEY_TILE % BLOCK == 0 and S % DSA_Q == 0 and DSA_Q % LANES == 0
    assert top_k % DSA_Q == 0, "a grid step must not mix queries with <= top_k and > top_k valid keys"
    blk = lambda n: pl.BlockSpec((1, DSA_Q, n), lambda b, j: (b, j, 0))
    whole = lambda n: pl.BlockSpec((1, S, n), lambda b, j: (b, 0, 0))
    ltri = jnp.tril(jnp.ones((BLOCK, BLOCK), jnp.bfloat16))
    return pl.pallas_call(
        functools.partial(_dsa_kernel, top_k=top_k),
        grid=(B, S // DSA_Q),
        in_specs=[blk(2 * LANES),
                  pl.BlockSpec((1, 8, DSA_Q), lambda b, j: (b, 0, j)),
                  blk(2 * LANES), whole(LANES), whole(LANES),
                  pl.BlockSpec((1, S // KEY_TILE, VT_ROWS, KEY_TILE), lambda b, j: (b, 0, 0, 0)),
                  _const_spec(ltri.shape)],
        out_specs=blk(DSA_WIDTH),
        out_shape=jax.ShapeDtypeStruct((B, S, DSA_WIDTH), jnp.bfloat16),
        scratch_shapes=[pltpu.VMEM((S, DSA_Q), jnp.float32),
                        pltpu.VMEM((KEY_TILE, N_DSA_HEADS * DSA_Q), jnp.float32),
                        pltpu.VMEM((KEY_TILE, N_DSA_HEADS * DSA_Q), jnp.float32)],
        compiler_params=_cparams(2),
        name="dsa",
    )(qi, wit, qd, ki, kd, vdt, ltri)


def _merge_kernel(x_ref, gates_ref, os_ref, od_ref, om_ref, wps_ref, wpd_ref, wpm_ref, wo_ref,
                  gm_ref, w1_ref, w2_ref, gf_ref, out_ref):
    f32 = jnp.float32
    x = x_ref[...]
    merged = None
    for b, (o_ref, w_ref) in enumerate(((os_ref, wps_ref), (od_ref, wpd_ref), (om_ref, wpm_ref))):
        y = jnp.dot(o_ref[...], w_ref[...], preferred_element_type=f32)
        y = y * gates_ref[:, b * D_MODEL:(b + 1) * D_MODEL].astype(f32)
        merged = y if merged is None else merged + y
    x1 = x + jnp.dot(merged.astype(jnp.bfloat16), wo_ref[...], preferred_element_type=f32)
    h = (x1 * _rms_scale(x1) * gm_ref[...]).astype(jnp.bfloat16)
    acc = x1
    FC = 1024
    for c in range(D_FF // FC):
        hid = jnp.maximum(jnp.dot(h, w1_ref[:, c * FC:(c + 1) * FC], preferred_element_type=f32), 0.0)
        hid = (hid * hid).astype(jnp.bfloat16)
        acc = acc + jnp.dot(hid, w2_ref[c * FC:(c + 1) * FC, :], preferred_element_type=f32)
    out_ref[...] = acc * _rms_scale(acc) * gf_ref[...]


def _merge(x2d, gates, o_swa, o_dsa, o_mem, w_proj_swa, w_proj_dsa, w_proj_mem, w_out,
           g_mlp, w_mlp_in, w_mlp_out, g_final):
    N = x2d.shape[0]
    T = TOK_TILE
    bf = jnp.bfloat16
    tok = lambda n: pl.BlockSpec((T, n), lambda i: (i, 0))
    ws = [w_proj_swa.astype(bf), w_proj_dsa.astype(bf), w_proj_mem.astype(bf), w_out.astype(bf)]
    w1, w2 = w_mlp_in.astype(bf), w_mlp_out.astype(bf)
    return pl.pallas_call(
        _merge_kernel,
        grid=(N // T,),
        in_specs=[tok(D_MODEL), tok(N_BRANCH * D_MODEL), tok(SWA_WIDTH), tok(DSA_WIDTH), tok(MEM_WIDTH)]
                 + [_const_spec(a.shape) for a in ws]
                 + [_const_spec((1, D_MODEL)), _const_spec(w1.shape), _const_spec(w2.shape),
                    _const_spec((1, D_MODEL))],
        out_specs=tok(D_MODEL),
        out_shape=jax.ShapeDtypeStruct((N, D_MODEL), jnp.float32),
        compiler_params=_cparams(1),
        name="merge",
    )(x2d, gates, o_swa, o_dsa, o_mem, *ws, g_mlp.reshape(1, D_MODEL), w1, w2,
      g_final.reshape(1, D_MODEL))


def kernel(x, mem, positions, g_mix, w_in, b_gate, sinks, g_mem, w_mem_kv, w_proj_swa, w_proj_dsa,
           w_proj_mem, w_out, g_mlp, w_mlp_in, w_mlp_out, g_final):
    B, S, D = x.shape
    assert g_mix.shape[0] == 1, "the final norm is fused into the single layer's merge kernel"
    top_k = min(TOPK_MAX, S // 4)
    for l in range(1):
        mk, mv = _memkv(mem, g_mem[l], w_mem_kv[l])
        (qd, kd, qi, ki, vdt, wit, gates, o_mem, o_swa) = _proj(
            x, positions, g_mix[l], w_in[l], b_gate[l], mk, mv, sinks[l])
        o_dsa = _dsa(qi, wit, qd, ki, kd, vdt, top_k)
        x = _merge(x.reshape(B * S, D), gates.reshape(B * S, -1), o_swa.reshape(B * S, -1),
                   o_dsa.reshape(B * S, -1), o_mem.reshape(B * S, -1), w_proj_swa[l], w_proj_dsa[l],
                   w_proj_mem[l], w_out[l], g_mlp[l], w_mlp_in[l], w_mlp_out[l], g_final).reshape(B, S, D)
    return x
```
